```python
import math
import jax, jax.numpy as jnp
from jax import lax
import numpy as np

D_MODEL = 1024
BATCH = 16
SEQ = 2048
DEPTH = 4

CHUNK = 64
Q_BLOCK = 128
D_PLE = 256
EPS = 1e-6

RET_HEADS = 4
RET_DK = 128
RET_DV = 128
RET_QK_WIDTH = RET_HEADS * RET_DK
RET_WIDTH = RET_HEADS * RET_DV
DIFF_HEADS = 4
DIFF_DH = 64
DIFF_DV = 2 * DIFF_DH
DIFF_QK_WIDTH = DIFF_HEADS * 2 * DIFF_DH
DIFF_WIDTH = DIFF_HEADS * DIFF_DV
MIX_WIDTH = RET_WIDTH + DIFF_WIDTH
IN_SPLITS = (RET_QK_WIDTH, RET_QK_WIDTH, RET_WIDTH, RET_WIDTH, DIFF_QK_WIDTH, DIFF_QK_WIDTH, DIFF_WIDTH)
IN_COLS = sum(IN_SPLITS)

N_GROUPS = 4
EXPERTS_PER_GROUP = 4
N_EXPERTS = N_GROUPS * EXPERTS_PER_GROUP
TOP_K_INNER = 2
D_EXPERT = 512

kernel_name = "hybrid_retention_diffattn_hmoe_ple"


def rmsnorm(x, g):
    xf = x.astype(jnp.float32)
    y = xf * lax.rsqrt(jnp.mean(xf * xf, axis=-1, keepdims=True) + EPS)
    return (y * g.astype(jnp.float32)).astype(x.dtype)


def retention(q, k, v, g):
    B, S, _ = q.shape
    nc = S // CHUNK
    dt = q.dtype
    f32 = jnp.float32

    def to_chunks(t, d):
        return t.astype(f32).reshape(B, nc, CHUNK, RET_HEADS, d).transpose(0, 1, 3, 2, 4)

    qc = to_chunks(q, RET_DK)
    kc = to_chunks(k, RET_DK) * (RET_DK ** -0.5)
    vc = to_chunks(v, RET_DV)
    log_g = jnp.log1p(-jnp.exp2(-5.0 - jnp.arange(RET_HEADS, dtype=f32)))
    pos = jnp.arange(CHUNK, dtype=f32)
    rel = pos[:, None] - pos[None, :]
    intra_decay = jnp.where(rel >= 0, jnp.exp(log_g[:, None, None] * jnp.maximum(rel, 0.0)), 0.0)
    q_decay = jnp.exp(log_g[:, None] * (pos + 1.0))
    k_decay = jnp.exp(log_g[:, None] * (CHUNK - 1.0 - pos))
    chunk_decay = jnp.exp(log_g * CHUNK)

    scores = jnp.einsum('bnhtd,bnhsd->bnhts', qc, kc) * intra_decay
    intra = jnp.einsum('bnhts,bnhse->bnhte', scores, vc)
    kv = jnp.einsum('bnhsd,bnhse->nbhde', kc * k_decay[:, :, None], vc)

    def step(state, kv_n):
        return chunk_decay[None, :, None, None] * state + kv_n, state

    _, prev = lax.scan(step, jnp.zeros((B, RET_HEADS, RET_DK, RET_DV), f32), kv)
    cross = jnp.einsum('bnhtd,nbhde->bnhte', qc * q_decay[:, :, None], prev)
    o = intra + cross
    o = o * lax.rsqrt(jnp.mean(o * o, axis=-1, keepdims=True) + EPS)
    o = o.transpose(0, 1, 3, 2, 4).reshape(B, S, RET_WIDTH)
    return (jax.nn.silu(g.astype(f32)) * o).astype(dt)


def diff_attention(q, k, v, lam, subln_g, lam_init):
    B, S, _ = q.shape
    nb = S // Q_BLOCK
    dt = q.dtype
    f32 = jnp.float32
    qh = q.astype(f32).reshape(B, S, DIFF_HEADS, 2, DIFF_DH) * (DIFF_DH ** -0.5)
    kh = k.astype(f32).reshape(B, S, DIFF_HEADS, 2, DIFF_DH)
    vh = v.astype(f32).reshape(B, S, DIFF_HEADS, DIFF_DV)
    qb = qh.reshape(B, nb, Q_BLOCK, DIFF_HEADS, 2, DIFF_DH).transpose(1, 0, 2, 3, 4, 5)
    slopes = jnp.exp2(-8.0 * jnp.arange(1, DIFF_HEADS + 1, dtype=f32) / DIFF_HEADS)
    key_pos = jnp.arange(S)

    def block(args):
        q_blk, start = args
        q_pos = start + jnp.arange(Q_BLOCK)
        s = jnp.einsum('bqhcd,bkhcd->bchqk', q_blk, kh)
        dist = jnp.abs(q_pos[:, None] - key_pos[None, :]).astype(f32)
        allowed = (key_pos[None, :] // CHUNK) <= (q_pos[:, None] // CHUNK)
        s = jnp.where(allowed, s - slopes[:, None, None] * dist, -jnp.inf)
        a = jax.nn.softmax(s, axis=-1)
        w = a[:, 0] - lam * a[:, 1]
        return jnp.einsum('bhqk,bkhe->bqhe', w, vh)

    o = lax.map(block, (qb, jnp.arange(nb) * Q_BLOCK))
    o = o.transpose(1, 0, 2, 3, 4).reshape(B, S, DIFF_HEADS, DIFF_DV)
    o = o * lax.rsqrt(jnp.mean(o * o, axis=-1, keepdims=True) + EPS) * subln_g.astype(f32)
    o = o * (1.0 - lam_init)
    return o.reshape(B, S, DIFF_WIDTH).astype(dt)


def hierarchical_moe(h, w_rg, b_rg, w_re, b_re, w_gate, w_up, w_down):
    B, S, D = h.shape
    f32 = jnp.float32
    t = h.reshape(B * S, D)
    lg = (t @ w_rg).astype(f32) + b_rg.astype(f32)
    g_idx = jnp.argmax(lg, axis=-1)
    g_w = jnp.take_along_axis(jax.nn.softmax(lg, axis=-1), g_idx[:, None], axis=-1)
    le = ((t @ w_re).astype(f32) + b_re.astype(f32)).reshape(-1, N_GROUPS, EXPERTS_PER_GROUP)
    le_sel = jnp.take_along_axis(le, g_idx[:, None, None], axis=1)[:, 0]
    top_v, top_i = lax.top_k(le_sel, TOP_K_INNER)
    top_w = jax.nn.softmax(top_v, axis=-1) * g_w
    w_in_group = jnp.sum(jax.nn.one_hot(top_i, EXPERTS_PER_GROUP, dtype=f32) * top_w[..., None], axis=1)
    combine = jax.nn.one_hot(g_idx, N_GROUPS, dtype=f32)[:, :, None] * w_in_group[:, None, :]
    combine = combine.reshape(-1, N_EXPERTS).astype(h.dtype)
    out = jnp.zeros_like(t)
    for gi in range(N_GROUPS):
        sl = slice(gi * EXPERTS_PER_GROUP, (gi + 1) * EXPERTS_PER_GROUP)
        hid = jax.nn.silu(jnp.einsum('td,edf->etf', t, w_gate[sl])) * jnp.einsum('td,edf->etf', t, w_up[sl])
        hid = hid * combine[:, sl].T[:, :, None]
        out = out + jnp.einsum('etf,efd->td', hid, w_down[sl])
    return out.reshape(B, S, D)


def setup_inputs(seed: int = 0) -> dict:
    key = jax.random.key(seed)
    ks = jax.random.split(key, 24)
    f32 = jnp.float32
    nrm = lambda k, shape, s: jax.random.normal(k, shape, f32) * s
    gain = lambda k, shape: 1.0 + 0.01 * jax.random.normal(k, shape, f32)
    return {
        "x": nrm(ks[0], (BATCH, SEQ, D_MODEL), 1.0),
        "p": nrm(ks[1], (DEPTH, BATCH, SEQ, D_PLE), 1.0),
        "norm_mix": gain(ks[2], (DEPTH, D_MODEL)),
        "w_in": nrm(ks[3], (DEPTH, D_MODEL, IN_COLS), D_MODEL ** -0.5),
        "lam_q1": nrm(ks[4], (DEPTH, DIFF_DH), 0.1),
        "lam_k1": nrm(ks[5], (DEPTH, DIFF_DH), 0.1),
        "lam_q2": nrm(ks[6], (DEPTH, DIFF_DH), 0.1),
        "lam_k2": nrm(ks[7], (DEPTH, DIFF_DH), 0.1),
        "diff_subln": gain(ks[8], (DEPTH, DIFF_DV)),
        "w_out": nrm(ks[9], (DEPTH, MIX_WIDTH, D_MODEL), MIX_WIDTH ** -0.5),
        "norm_ffn": gain(ks[10], (DEPTH, D_MODEL)),
        "w_router_group": nrm(ks[11], (DEPTH, D_MODEL, N_GROUPS), D_MODEL ** -0.5),
        "b_router_group": nrm(ks[12], (DEPTH, N_GROUPS), 0.01),
        "w_router_expert": nrm(ks[13], (DEPTH, D_MODEL, N_EXPERTS), D_MODEL ** -0.5),
        "b_router_expert": nrm(ks[14], (DEPTH, N_EXPERTS), 0.01),
        "w_gate": nrm(ks[15], (DEPTH, N_EXPERTS, D_MODEL, D_EXPERT), D_MODEL ** -0.5),
        "w_up": nrm(ks[16], (DEPTH, N_EXPERTS, D_MODEL, D_EXPERT), D_MODEL ** -0.5),
        "w_down": nrm(ks[17], (DEPTH, N_EXPERTS, D_EXPERT, D_MODEL), D_EXPERT ** -0.5),
        "norm_ple": gain(ks[18], (DEPTH, D_MODEL)),
        "w_ple_gate": nrm(ks[19], (DEPTH, D_MODEL, D_MODEL), D_MODEL ** -0.5),
        "w_ple_proj": nrm(ks[20], (DEPTH, D_PLE, D_MODEL), D_PLE ** -0.5),
        "norm_final": gain(ks[21], (D_MODEL,)),
    }


def reference(x, p, norm_mix, w_in, lam_q1, lam_k1, lam_q2, lam_k2, diff_subln, w_out,
              norm_ffn, w_router_group, b_router_group, w_router_expert, b_router_expert,
              w_gate, w_up, w_down, norm_ple, w_ple_gate, w_ple_proj, norm_final):
    f32 = jnp.float32
    for i in range(DEPTH):
        h = rmsnorm(x, norm_mix[i])
        proj = h @ w_in[i]
        rq, rk, rv, rg, dq, dk, dv = jnp.split(proj, np.cumsum(IN_SPLITS)[:-1].tolist(), axis=-1)
        lam_init = 0.8 - 0.6 * math.exp(-0.3 * i)
        lam = (jnp.exp(jnp.sum(lam_q1[i].astype(f32) * lam_k1[i].astype(f32)))
               - jnp.exp(jnp.sum(lam_q2[i].astype(f32) * lam_k2[i].astype(f32))) + lam_init)
        mix = jnp.concatenate([retention(rq, rk, rv, rg),
                               diff_attention(dq, dk, dv, lam, diff_subln[i], lam_init)], axis=-1)
        x = x + mix @ w_out[i]
        h = rmsnorm(x, norm_ffn[i])
        x = x + hierarchical_moe(h, w_router_group[i], b_router_group[i], w_router_expert[i],
                                 b_router_expert[i], w_gate[i], w_up[i], w_down[i])
        gate = jax.nn.sigmoid(rmsnorm(x, norm_ple[i]) @ w_ple_gate[i])
        x = x + gate * (p[i] @ w_ple_proj[i])
    return rmsnorm(x, norm_final)
```

```python
import functools
import math

import jax
import jax.numpy as jnp
from jax import lax
from jax.experimental import pallas as pl
from jax.experimental.pallas import tpu as pltpu

f32 = jnp.float32
bf16 = jnp.bfloat16
i32 = jnp.int32

D_MODEL = 1024
DEPTH = 4
CHUNK = 64
D_PLE = 256
EPS = 1e-6
RET_HEADS = 4
RET_DK = 128
DIFF_HEADS = 4
DIFF_DH = 64
HEAD_W = 128
SPLIT_W = 512
IN_COLS = 7 * SPLIT_W
N_GROUPS = 4
EXPERTS_PER_GROUP = 4
N_EXPERTS = 16
D_EXPERT = 512
N_PAIRS = 6
N_BUCKETS = N_GROUPS * N_PAIRS
BUCKET_ROWS = 32

LANES = 128
ROUTE_W = LANES
ROW_W = D_MODEL + ROUTE_W

TM = 512
RET_BLOCK = 256
TQ = 256
TM_MOE = 512
TM_PERM = 256
VMEM_LIMIT = 56 * 1024 * 1024
NEG_BIG = -1e30


def _rms(x, g):
    return x * lax.rsqrt(jnp.mean(x * x, axis=-1, keepdims=True) + EPS) * g


def _dot(a, b):
    return jnp.dot(a, b, preferred_element_type=f32)


def _dot_nt(a, b):
    return lax.dot_general(a, b, (((1,), (1,)), ((), ())), preferred_element_type=f32)


def _dot_tn(a, b):
    return lax.dot_general(a, b, (((0,), (0,)), ((), ())), preferred_element_type=f32)


def _params(*sem):
    return pltpu.CompilerParams(dimension_semantics=sem, vmem_limit_bytes=VMEM_LIMIT)


def _inproj_body(x_ref, g_ref, w_ref, o_ref):
    h = _rms(x_ref[...], g_ref[...]).astype(bf16)
    for n in range(IN_COLS // SPLIT_W):
        cols = slice(n * SPLIT_W, (n + 1) * SPLIT_W)
        o_ref[:, cols] = _dot(h, w_ref[:, cols]).astype(bf16)


def _inproj(x, g, w):
    t = x.shape[0]
    return pl.pallas_call(
        _inproj_body,
        out_shape=jax.ShapeDtypeStruct((t, IN_COLS), bf16),
        grid=(t // TM,),
        in_specs=[pl.BlockSpec((TM, D_MODEL), lambda i: (i, 0)),
                  pl.BlockSpec((1, D_MODEL), lambda i: (0, 0)),
                  pl.BlockSpec((D_MODEL, IN_COLS), lambda i: (0, 0))],
        out_specs=pl.BlockSpec((TM, IN_COLS), lambda i: (i, 0)),
        compiler_params=_params("parallel"),
        name="inproj",
    )(x, g, w)


def _retention_body(q_ref, k_ref, v_ref, g_ref, o_ref, state_ref):
    @pl.when(pl.program_id(1) == 0)
    def _():
        state_ref[...] = jnp.zeros_like(state_ref)

    n = RET_BLOCK
    rel = (lax.broadcasted_iota(i32, (n, n), 0) - lax.broadcasted_iota(i32, (n, n), 1)).astype(f32)
    pos = lax.broadcasted_iota(i32, (n, HEAD_W), 0).astype(f32)
    scale = RET_DK ** -0.5
    for h in range(RET_HEADS):
        log_g = math.log1p(-(2.0 ** (-5.0 - h)))
        cols = slice(h * HEAD_W, (h + 1) * HEAD_W)
        q = q_ref[:, cols]
        k = k_ref[:, cols]
        v = v_ref[:, cols]
        decay = jnp.where(rel >= 0, jnp.exp(log_g * jnp.maximum(rel, 0.0)), 0.0) * scale
        scores = _dot_nt(q, k) * decay
        intra = _dot(scores.astype(bf16), v)
        state = state_ref[h]
        cross = _dot(q, state.astype(bf16)) * jnp.exp(log_g * (pos + 1.0))
        k_dec = (k.astype(f32) * (jnp.exp(log_g * (n - 1.0 - pos)) * scale)).astype(bf16)
        state_ref[h] = math.exp(log_g * n) * state + _dot_tn(k_dec, v)
        o = intra + cross
        o = o * lax.rsqrt(jnp.mean(o * o, axis=-1, keepdims=True) + EPS)
        gate = g_ref[:, cols].astype(f32)
        o_ref[:, cols] = (gate * jax.nn.sigmoid(gate) * o).astype(bf16)


def _retention(proj, batch, seq):
    nb = seq // RET_BLOCK
    spec = lambda c: pl.BlockSpec((RET_BLOCK, SPLIT_W), lambda b, j: (b * nb + j, c))
    return pl.pallas_call(
        _retention_body,
        out_shape=jax.ShapeDtypeStruct((batch * seq, SPLIT_W), bf16),
        grid=(batch, nb),
        in_specs=[spec(0), spec(1), spec(2), spec(3)],
        out_specs=pl.BlockSpec((RET_BLOCK, SPLIT_W), lambda b, j: (b * nb + j, 0)),
        scratch_shapes=[pltpu.VMEM((RET_HEADS, RET_DK, HEAD_W), f32)],
        compiler_params=_params("parallel", "arbitrary"),
        name="retention",
    )(proj, proj, proj, proj)


def _softmax_step(q, kb, vb, bias, off, m, l, acc):
    t = _dot_nt(q, kb) + bias
    m_new = jnp.maximum(m, jnp.max(t, axis=-1, keepdims=True) + off)
    p = jnp.exp(t + (off - m_new))
    alpha = jnp.exp(m - m_new)
    l = alpha * l + jnp.sum(p, axis=-1, keepdims=True)
    acc = alpha * acc + _dot(p.astype(bf16), vb)
    return m_new, l, acc


def _diffattn_body(q_ref, k_ref, v_ref, lamv_ref, subln_ref, const_ref, o_ref):
    h = pl.program_id(1)
    qi = pl.program_id(2)
    slope = jnp.where(h == 0, 2.0 ** -2, jnp.where(h == 1, 2.0 ** -4, jnp.where(h == 2, 2.0 ** -6, 2.0 ** -8)))
    slope = slope.astype(f32)
    q = q_ref[...] * jnp.asarray(DIFF_DH ** -0.5, bf16)
    qs = (q[:, :DIFF_DH], q[:, DIFF_DH:])
    row = lax.broadcasted_iota(i32, (TQ, TQ), 0)
    col = lax.broadcasted_iota(i32, (TQ, TQ), 1)
    rel = (row - col).astype(f32)

    def kv_block(j, carry):
        start = pl.multiple_of(j * TQ, TQ)
        kb = k_ref[pl.ds(start, TQ), :]
        vb = v_ref[pl.ds(start, TQ), :]
        off = -slope * ((qi - j) * TQ).astype(f32)
        out = []
        for c in range(2):
            m, l, acc = carry[3 * c:3 * c + 3]
            out.extend(_softmax_step(qs[c], kb[:, c * DIFF_DH:(c + 1) * DIFF_DH], vb, -slope * rel, off, m, l, acc))
        return tuple(out)

    init = []
    for _ in range(2):
        init.extend([jnp.full((TQ, 1), NEG_BIG, f32), jnp.zeros((TQ, 1), f32), jnp.zeros((TQ, HEAD_W), f32)])
    carry = lax.fori_loop(0, qi, kv_block, tuple(init))

    start = pl.multiple_of(qi * TQ, TQ)
    kb = k_ref[pl.ds(start, TQ), :]
    vb = v_ref[pl.ds(start, TQ), :]
    allowed = (col // CHUNK) <= (row // CHUNK)
    bias = jnp.where(allowed, -slope * jnp.abs(rel), NEG_BIG)
    outs = []
    for c in range(2):
        m, l, acc = carry[3 * c:3 * c + 3]
        m, l, acc = _softmax_step(qs[c], kb[:, c * DIFF_DH:(c + 1) * DIFF_DH], vb, bias, jnp.zeros((), f32), m, l, acc)
        outs.append(acc / l)

    lamv = lamv_ref[...]
    lam_init = const_ref[0:1, 0:1]
    lam = (jnp.exp(jnp.sum(lamv[0:1] * lamv[1:2], axis=-1, keepdims=True))
           - jnp.exp(jnp.sum(lamv[2:3] * lamv[3:4], axis=-1, keepdims=True)) + lam_init)
    o = outs[0] - lam * outs[1]
    o = o * lax.rsqrt(jnp.mean(o * o, axis=-1, keepdims=True) + EPS) * subln_ref[...]
    o_ref[...] = (o * (1.0 - lam_init)).astype(bf16)


def _diffattn(proj, lamv, subln, consts, batch, seq):
    nq = seq // TQ
    q0, k0, v0 = (4 * SPLIT_W // HEAD_W, 5 * SPLIT_W // HEAD_W, 6 * SPLIT_W // HEAD_W)
    return pl.pallas_call(
        _diffattn_body,
        out_shape=jax.ShapeDtypeStruct((batch * seq, SPLIT_W), bf16),
        grid=(batch, DIFF_HEADS, nq),
        in_specs=[pl.BlockSpec((TQ, HEAD_W), lambda b, h, i: (b * nq + i, q0 + h)),
                  pl.BlockSpec((seq, HEAD_W), lambda b, h, i: (b, k0 + h)),
                  pl.BlockSpec((seq, HEAD_W), lambda b, h, i: (b, v0 + h)),
                  pl.BlockSpec((8, LANES), lambda b, h, i: (0, 0)),
                  pl.BlockSpec((1, HEAD_W), lambda b, h, i: (0, 0)),
                  pl.BlockSpec((1, LANES), lambda b, h, i: (0, 0))],
        out_specs=pl.BlockSpec((TQ, HEAD_W), lambda b, h, i: (b * nq + i, h)),
        compiler_params=_params("parallel", "parallel", "arbitrary"),
        name="diffattn",
    )(proj, proj, proj, lamv, subln, consts)


def _route_rows(lt):
    lg = [lt[r:r + 1, :] for r in range(N_GROUPS)]
    best, gidx = lg[0], jnp.zeros(lg[0].shape, i32)
    for r in range(1, N_GROUPS):
        upd = lg[r] > best
        gidx = jnp.where(upd, r, gidx)
        best = jnp.where(upd, lg[r], best)
    den = sum(jnp.exp(v - best) for v in lg)
    g_w = 1.0 / den

    def sel(e):
        rows = [lt[8 + 4 * g + e:9 + 4 * g + e, :] for g in range(N_GROUPS)]
        return jnp.where(gidx == 0, rows[0], jnp.where(gidx == 1, rows[1], jnp.where(gidx == 2, rows[2], rows[3])))

    le = [sel(e) for e in range(EXPERTS_PER_GROUP)]
    v1, i1 = le[0], jnp.zeros(le[0].shape, i32)
    for e in range(1, EXPERTS_PER_GROUP):
        upd = le[e] > v1
        i1 = jnp.where(upd, e, i1)
        v1 = jnp.where(upd, le[e], v1)
    rest = [jnp.where(i1 == e, -jnp.inf, le[e]) for e in range(EXPERTS_PER_GROUP)]
    v2, i2 = rest[0], jnp.zeros(le[0].shape, i32)
    for e in range(1, EXPERTS_PER_GROUP):
        upd = rest[e] > v2
        i2 = jnp.where(upd, e, i2)
        v2 = jnp.where(upd, rest[e], v2)
    e2 = jnp.exp(v2 - v1)
    w1 = (1.0 / (1.0 + e2)) * g_w
    w2 = (e2 / (1.0 + e2)) * g_w
    first_low = i1 < i2
    lo = jnp.minimum(i1, i2)
    hi = jnp.maximum(i1, i2)
    pair = jnp.where(lo == 0, hi - 1, jnp.where(lo == 1, hi + 1, 5))
    return gidx * N_PAIRS + pair, jnp.where(first_low, w1, w2), jnp.where(first_low, w2, w1)


def _outproj_body(ret_ref, diff_ref, wo_ref, x_ref, g_ref, wr_ref, rb_ref,
                  xw_ref, route_ref, cnt_ref, run_ref):
    @pl.when(pl.program_id(0) == 0)
    def _():
        run_ref[...] = jnp.zeros_like(run_ref)

    x1 = x_ref[...] + _dot(ret_ref[...], wo_ref[:SPLIT_W, :]) + _dot(diff_ref[...], wo_ref[SPLIT_W:, :])
    xw_ref[:, :D_MODEL] = x1

    h = _rms(x1, g_ref[...])
    h_hi = h.astype(bf16)
    h_lo = (h - h_hi.astype(f32)).astype(bf16)
    r1 = _dot(h_hi, wr_ref[...])
    logits = r1[:, :LANES] + (r1[:, LANES:] + _dot(h_lo, wr_ref[:, :LANES])) + rb_ref[...]
    bucket, w_low, w_high = _route_rows(logits.T)

    sub = lax.broadcasted_iota(i32, (LANES, TM), 0)
    wt = jnp.where(sub == 0, w_low, jnp.where(sub == 1, w_high, 0.0))
    xw_ref[:, D_MODEL:] = wt.T

    onehot = lax.broadcasted_iota(i32, (BUCKET_ROWS, TM), 0) == bucket
    oh_b = jnp.where(onehot, 1.0, 0.0).astype(bf16)
    earlier = lax.broadcasted_iota(i32, (TM, TM), 0) < lax.broadcasted_iota(i32, (TM, TM), 1)
    before = _dot(oh_b, jnp.where(earlier, 1.0, 0.0).astype(bf16))
    run = run_ref[...]
    oh_f = jnp.where(onehot, 1.0, 0.0)
    rank = jnp.sum(oh_f * (before + run[:, 0:1]), axis=0, keepdims=True).astype(i32)
    sub8 = lax.broadcasted_iota(i32, (8, TM), 0)
    route_ref[0] = jnp.where(sub8 == 0, bucket, jnp.where(sub8 == 1, rank, 0))
    run = run + jnp.sum(oh_f, axis=1, keepdims=True)
    run_ref[...] = run
    cnt_ref[...] = run


def _outproj(ret, diff, wo, x, g, wr, rb):
    t = x.shape[0]
    nt = t // TM
    return pl.pallas_call(
        _outproj_body,
        out_shape=(jax.ShapeDtypeStruct((t, ROW_W), f32),
                   jax.ShapeDtypeStruct((nt, 8, TM), i32),
                   jax.ShapeDtypeStruct((BUCKET_ROWS, LANES), f32)),
        grid=(nt,),
        in_specs=[pl.BlockSpec((TM, SPLIT_W), lambda i: (i, 0)),
                  pl.BlockSpec((TM, SPLIT_W), lambda i: (i, 0)),
                  pl.BlockSpec((D_MODEL, D_MODEL), lambda i: (0, 0)),
                  pl.BlockSpec((TM, D_MODEL), lambda i: (i, 0)),
                  pl.BlockSpec((1, D_MODEL), lambda i: (0, 0)),
                  pl.BlockSpec((D_MODEL, 2 * LANES), lambda i: (0, 0)),
                  pl.BlockSpec((1, LANES), lambda i: (0, 0))],
        out_specs=(pl.BlockSpec((TM, ROW_W), lambda i: (i, 0)),
                   pl.BlockSpec((1, 8, TM), lambda i: (i, 0, 0)),
                   pl.BlockSpec((BUCKET_ROWS, LANES), lambda i: (0, 0))),
        scratch_shapes=[pltpu.VMEM((BUCKET_ROWS, LANES), f32)],
        compiler_params=_params("arbitrary"),
        name="outproj_router",
    )(ret, diff, wo, x, g, wr, rb)


def _row_copy_out(src_ref, dst_ref, pos_ref, sem, base):
    def copy(r):
        return pltpu.make_async_copy(src_ref.at[pl.ds(r, 1)], dst_ref.at[pl.ds(pos_ref[base + r], 1)], sem)

    def start(r, c):
        copy(r).start()
        return c

    def wait(r, c):
        copy(r).wait()
        return c

    lax.fori_loop(0, TM_PERM, start, 0)
    lax.fori_loop(0, TM_PERM, wait, 0)


def _dispatch_body(pos_ref, x_ref, init_ref, o_ref, sem):
    del init_ref
    _row_copy_out(x_ref, o_ref, pos_ref, sem.at[0], pl.program_id(0) * TM_PERM)


def _dispatch(pos, xw, rows_out):
    t = xw.shape[0]
    return pl.pallas_call(
        _dispatch_body,
        out_shape=jax.ShapeDtypeStruct((rows_out, ROW_W), f32),
        grid_spec=pltpu.PrefetchScalarGridSpec(
            num_scalar_prefetch=1,
            grid=(t // TM_PERM,),
            in_specs=[pl.BlockSpec((TM_PERM, ROW_W), lambda i, pos: (i, 0)),
                      pl.BlockSpec(memory_space=pl.ANY)],
            out_specs=pl.BlockSpec(memory_space=pl.ANY),
            scratch_shapes=[pltpu.SemaphoreType.DMA((1,))]),
        input_output_aliases={2: 0},
        compiler_params=_params("arbitrary"),
        name="dispatch",
    )(pos, xw, jnp.zeros((rows_out, ROW_W), f32))


def _combine_body(pos_ref, src_ref, o_ref, sem):
    base = pl.program_id(0) * TM_PERM

    def copy(r):
        return pltpu.make_async_copy(src_ref.at[pl.ds(pos_ref[base + r], 1)], o_ref.at[pl.ds(r, 1)], sem.at[0])

    def start(r, c):
        copy(r).start()
        return c

    def wait(r, c):
        copy(r).wait()
        return c

    lax.fori_loop(0, TM_PERM, start, 0)
    lax.fori_loop(0, TM_PERM, wait, 0)


def _combine(pos, ys, t):
    return pl.pallas_call(
        _combine_body,
        out_shape=jax.ShapeDtypeStruct((t, D_MODEL), f32),
        grid_spec=pltpu.PrefetchScalarGridSpec(
            num_scalar_prefetch=1,
            grid=(t // TM_PERM,),
            in_specs=[pl.BlockSpec(memory_space=pl.ANY)],
            out_specs=pl.BlockSpec((TM_PERM, D_MODEL), lambda i, pos: (i, 0)),
            scratch_shapes=[pltpu.SemaphoreType.DMA((1,))]),
        compiler_params=_params("arbitrary"),
        name="combine",
    )(pos, ys)


def _moe_body(e_lo_ref, e_hi_ref, valid_ref, xs_ref, g_ref, wg1, wu1, wd1, wg2, wu2, wd2, o_ref):
    nv = valid_ref[pl.program_id(0)]

    @pl.when(nv > 0)
    def _():
        ok = lax.broadcasted_iota(i32, (TM_MOE, 1), 0) < nv
        x = jnp.where(ok, xs_ref[:, :D_MODEL], 0.0)
        w_lo = jnp.where(ok, xs_ref[:, D_MODEL:D_MODEL + 1], 0.0)
        w_hi = jnp.where(ok, xs_ref[:, D_MODEL + 1:D_MODEL + 2], 0.0)
        h = _rms(x, g_ref[...]).astype(bf16)
        hid1 = jax.nn.silu(_dot(h, wg1[0])) * _dot(h, wu1[0]) * w_lo
        hid2 = jax.nn.silu(_dot(h, wg2[0])) * _dot(h, wu2[0]) * w_hi
        o_ref[...] = x + (_dot(hid1.astype(bf16), wd1[0]) + _dot(hid2.astype(bf16), wd2[0]))

    @pl.when(nv == 0)
    def _():
        o_ref[...] = jnp.zeros_like(o_ref)


def _moe(e_lo, e_hi, valid, xs, g, wg, wu, wd):
    n_tiles = xs.shape[0] // TM_MOE
    up = lambda sel: pl.BlockSpec((1, D_MODEL, D_EXPERT), lambda i, lo, hi, nv: ((lo, hi)[sel][i], 0, 0))
    down = lambda sel: pl.BlockSpec((1, D_EXPERT, D_MODEL), lambda i, lo, hi, nv: ((lo, hi)[sel][i], 0, 0))
    return pl.pallas_call(
        _moe_body,
        out_shape=jax.ShapeDtypeStruct((xs.shape[0], D_MODEL), f32),
        grid_spec=pltpu.PrefetchScalarGridSpec(
            num_scalar_prefetch=3,
            grid=(n_tiles,),
            in_specs=[pl.BlockSpec((TM_MOE, ROW_W), lambda i, lo, hi, nv: (i, 0)),
                      pl.BlockSpec((1, D_MODEL), lambda i, lo, hi, nv: (0, 0)),
                      up(0), up(0), down(0), up(1), up(1), down(1)],
            out_specs=pl.BlockSpec((TM_MOE, D_MODEL), lambda i, lo, hi, nv: (i, 0))),
        compiler_params=_params("arbitrary"),
        name="moe",
    )(e_lo, e_hi, valid, xs, g, wg, wu, wd, wg, wu, wd)


def _sort_plan(route, counts_f, n_tiles):
    bucket = route[:, 0, :].reshape(-1)
    rank = route[:, 1, :].reshape(-1)
    counts = counts_f[:N_BUCKETS, 0].astype(i32)
    tiles_per = (counts + TM_MOE - 1) // TM_MOE
    tile_end = jnp.cumsum(tiles_per)
    tile_start = tile_end - tiles_per
    pos = (tile_start * TM_MOE)[bucket] + rank
    tile = jnp.arange(n_tiles, dtype=i32)
    used = tile < tile_end[-1]
    tb = jnp.minimum(jnp.searchsorted(tile_end, tile, side="right").astype(i32), N_BUCKETS - 1)
    last = jnp.minimum(jnp.searchsorted(tile_end, tile_end[-1] - 1, side="right").astype(i32), N_BUCKETS - 1)
    tb = jnp.where(used, tb, last)
    valid = jnp.where(used, jnp.clip(counts[tb] - (tile - tile_start[tb]) * TM_MOE, 0, TM_MOE), 0).astype(i32)
    pair_lo = jnp.array([0, 0, 0, 1, 1, 2], i32)
    pair_hi = jnp.array([1, 2, 3, 2, 3, 3], i32)
    grp = tb // N_PAIRS
    e_lo = grp * EXPERTS_PER_GROUP + pair_lo[tb % N_PAIRS]
    e_hi = grp * EXPERTS_PER_GROUP + pair_hi[tb % N_PAIRS]
    return pos.astype(i32), e_lo, e_hi, valid


def _ple_body(x_ref, p_ref, g_ref, wg_ref, wp_ref, gf_ref, o_ref, *, final):
    x = x_ref[...]
    gate = jax.nn.sigmoid(_dot(_rms(x, g_ref[...]).astype(bf16), wg_ref[...]))
    y = x + gate * _dot(p_ref[...].astype(bf16), wp_ref[...])
    o_ref[...] = _rms(y, gf_ref[...]) if final else y


def _ple(x, p, g, wg, wp, gf, final):
    t = x.shape[0]
    return pl.pallas_call(
        functools.partial(_ple_body, final=final),
        out_shape=jax.ShapeDtypeStruct((t, D_MODEL), f32),
        grid=(t // TM,),
        in_specs=[pl.BlockSpec((TM, D_MODEL), lambda i: (i, 0)),
                  pl.BlockSpec((TM, D_PLE), lambda i: (i, 0)),
                  pl.BlockSpec((1, D_MODEL), lambda i: (0, 0)),
                  pl.BlockSpec((D_MODEL, D_MODEL), lambda i: (0, 0)),
                  pl.BlockSpec((D_PLE, D_MODEL), lambda i: (0, 0)),
                  pl.BlockSpec((1, D_MODEL), lambda i: (0, 0))],
        out_specs=pl.BlockSpec((TM, D_MODEL), lambda i: (i, 0)),
        compiler_params=_params("parallel"),
        name="ple_final" if final else "ple",
    )(x, p, g, wg, wp, gf)


def _router_weights(w_rg, b_rg, w_re, b_re):
    w = jnp.zeros((D_MODEL, LANES), f32)
    w = w.at[:, :N_GROUPS].set(w_rg).at[:, 8:8 + N_EXPERTS].set(w_re)
    hi = w.astype(bf16)
    lo = (w - hi.astype(f32)).astype(bf16)
    b = jnp.zeros((1, LANES), f32).at[0, :N_GROUPS].set(b_rg).at[0, 8:8 + N_EXPERTS].set(b_re)
    return jnp.concatenate([hi, lo], axis=1), b


def kernel(x, p, norm_mix, w_in, lam_q1, lam_k1, lam_q2, lam_k2, diff_subln, w_out, norm_ffn,
           w_router_group, b_router_group, w_router_expert, b_router_expert, w_gate, w_up, w_down,
           norm_ple, w_ple_gate, w_ple_proj, norm_final):
    batch, seq, _ = x.shape
    t = batch * seq
    n_tiles = t // TM_MOE + N_BUCKETS
    xf = x.reshape(t, D_MODEL)
    row = lambda v: v.reshape(1, -1).astype(f32)
    for i in range(DEPTH):
        lam_init = 0.8 - 0.6 * math.exp(-0.3 * i)
        proj = _inproj(xf, row(norm_mix[i]), w_in[i].astype(bf16))
        ret = _retention(proj, batch, seq)
        lamv = jnp.zeros((8, LANES), f32)
        for r, v in enumerate((lam_q1[i], lam_k1[i], lam_q2[i], lam_k2[i])):
            lamv = lamv.at[r, :DIFF_DH].set(v.astype(f32))
        consts = jnp.full((1, LANES), lam_init, f32)
        diff = _diffattn(proj, lamv, row(diff_subln[i]), consts, batch, seq)
        wr, rb = _router_weights(w_router_group[i], b_router_group[i], w_router_expert[i], b_router_expert[i])
        xw, route, counts = _outproj(ret, diff, w_out[i].astype(bf16), xf, row(norm_ffn[i]), wr, rb)
        pos, e_lo, e_hi, valid = _sort_plan(route, counts, n_tiles)
        xs = _dispatch(pos, xw, n_tiles * TM_MOE)
        ys = _moe(e_lo, e_hi, valid, xs, row(norm_ffn[i]),
                  w_gate[i].astype(bf16), w_up[i].astype(bf16), w_down[i].astype(bf16))
        x2 = _combine(pos, ys, t)
        xf = _ple(x2, p[i].reshape(t, D_PLE), row(norm_ple[i]), w_ple_gate[i].astype(bf16),
                  w_ple_proj[i].astype(bf16), row(norm_final), final=(i == DEPTH - 1))
    return xf.reshape(batch, seq, D_MODEL)
```

```python
import functools
import math

import jax
import jax.numpy as jnp
from jax import lax
from jax.experimental import pallas as pl
from jax.experimental.pallas import tpu as pltpu

f32 = jnp.float32
bf16 = jnp.bfloat16
i32 = jnp.int32

D_MODEL = 1024
DEPTH = 4
CHUNK = 64
D_PLE = 256
EPS = 1e-6
RET_HEADS = 4
RET_DK = 128
DIFF_HEADS = 4
DIFF_DH = 64
HEAD_W = 128
SPLIT_W = 512
IN_COLS = 7 * SPLIT_W
N_GROUPS = 4
EXPERTS_PER_GROUP = 4
N_EXPERTS = 16
D_EXPERT = 512
N_PAIRS = 6
N_BUCKETS = N_GROUPS * N_PAIRS
BUCKET_ROWS = 32

LANES = 128
ROUTE_W = LANES
ROW_W = D_MODEL + ROUTE_W

TM = 512
RET_BLOCK = 256
TQ = 256
STRIP = 64
TM_MOE = 512
TM_PERM = 256
VMEM_LIMIT = 56 * 1024 * 1024
NEG_BIG = -1e30


def _rms(x, g):
    return x * lax.rsqrt(jnp.mean(x * x, axis=-1, keepdims=True) + EPS) * g


def _dot(a, b):
    return jnp.dot(a, b, preferred_element_type=f32)


def _dot_nt(a, b):
    return lax.dot_general(a, b, (((1,), (1,)), ((), ())), preferred_element_type=f32)


def _dot_tn(a, b):
    return lax.dot_general(a, b, (((0,), (0,)), ((), ())), preferred_element_type=f32)


def _params(*sem):
    return pltpu.CompilerParams(dimension_semantics=sem, vmem_limit_bytes=VMEM_LIMIT)


def _inproj_body(x_ref, g_ref, w_ref, o_ref):
    h = _rms(x_ref[...], g_ref[...]).astype(bf16)
    for n in range(IN_COLS // SPLIT_W):
        cols = slice(n * SPLIT_W, (n + 1) * SPLIT_W)
        o_ref[:, cols] = _dot(h, w_ref[:, cols]).astype(bf16)


def _inproj(x, g, w):
    t = x.shape[0]
    return pl.pallas_call(
        _inproj_body,
        out_shape=jax.ShapeDtypeStruct((t, IN_COLS), bf16),
        grid=(t // TM,),
        in_specs=[pl.BlockSpec((TM, D_MODEL), lambda i: (i, 0)),
                  pl.BlockSpec((1, D_MODEL), lambda i: (0, 0)),
                  pl.BlockSpec((D_MODEL, IN_COLS), lambda i: (0, 0))],
        out_specs=pl.BlockSpec((TM, IN_COLS), lambda i: (i, 0)),
        compiler_params=_params("parallel"),
        name="inproj",
    )(x, g, w)


def _retention_body(q_ref, k_ref, v_ref, g_ref, o_ref, state_ref):
    @pl.when(pl.program_id(1) == 0)
    def _():
        state_ref[...] = jnp.zeros_like(state_ref)

    n = RET_BLOCK
    rel = (lax.broadcasted_iota(i32, (n, n), 0) - lax.broadcasted_iota(i32, (n, n), 1)).astype(f32)
    pos = lax.broadcasted_iota(i32, (n, HEAD_W), 0).astype(f32)
    scale = RET_DK ** -0.5
    for h in range(RET_HEADS):
        log_g = math.log1p(-(2.0 ** (-5.0 - h)))
        cols = slice(h * HEAD_W, (h + 1) * HEAD_W)
        q = q_ref[:, cols]
        k = k_ref[:, cols]
        v = v_ref[:, cols]
        decay = jnp.where(rel >= 0, jnp.exp(log_g * jnp.maximum(rel, 0.0)), 0.0) * scale
        scores = _dot_nt(q, k) * decay
        intra = _dot(scores.astype(bf16), v)
        state = state_ref[h]
        cross = _dot(q, state.astype(bf16)) * jnp.exp(log_g * (pos + 1.0))
        k_dec = (k.astype(f32) * (jnp.exp(log_g * (n - 1.0 - pos)) * scale)).astype(bf16)
        state_ref[h] = math.exp(log_g * n) * state + _dot_tn(k_dec, v)
        o = intra + cross
        o = o * lax.rsqrt(jnp.mean(o * o, axis=-1, keepdims=True) + EPS)
        gate = g_ref[:, cols].astype(f32)
        o_ref[:, cols] = (gate * jax.nn.sigmoid(gate) * o).astype(bf16)


def _retention(proj, batch, seq):
    nb = seq // RET_BLOCK
    spec = lambda c: pl.BlockSpec((RET_BLOCK, SPLIT_W), lambda b, j: (b * nb + j, c))
    return pl.pallas_call(
        _retention_body,
        out_shape=jax.ShapeDtypeStruct((batch * seq, SPLIT_W), bf16),
        grid=(batch, nb),
        in_specs=[spec(0), spec(1), spec(2), spec(3)],
        out_specs=pl.BlockSpec((RET_BLOCK, SPLIT_W), lambda b, j: (b * nb + j, 0)),
        scratch_shapes=[pltpu.VMEM((RET_HEADS, RET_DK, HEAD_W), f32)],
        compiler_params=_params("parallel", "arbitrary"),
        name="retention",
    )(proj, proj, proj, proj)


_STRIPS = [slice(s * STRIP, (s + 1) * STRIP) for s in range(TQ // STRIP)]
_fold8 = lambda a: a.reshape(STRIP // 8, 8, TQ)


def _raw_scores(k_ref, blk, q_cat, raw_ref):
    for rows in _STRIPS:
        kb = k_ref[pl.ds(pl.multiple_of(blk * TQ + rows.start, STRIP), STRIP), :]
        raw_ref[rows, :] = _dot(kb, q_cat)


def _bias_and_max(raw_ref, bias_ref, off, m, t_ref):
    out = []
    for c in range(2):
        m8 = None
        for rows in _STRIPS:
            t = raw_ref[rows, c * TQ:(c + 1) * TQ] + bias_ref[rows, :]
            t_ref[c, rows, :] = t
            g = jnp.max(_fold8(t), axis=0)
            m8 = g if m8 is None else jnp.maximum(m8, g)
        out.append(jnp.maximum(m[c], jnp.max(m8, axis=0, keepdims=True) + off))
    return out


def _accumulate(t_ref, off, m_old, m_new, l, v_t, p_ref, acc_ref):
    out = []
    for c in range(2):
        shift = off - m_new[c]
        l8 = jnp.zeros((8, TQ), f32)
        for rows in _STRIPS:
            p = jnp.exp(t_ref[c, rows, :] + shift)
            l8 = l8 + jnp.sum(_fold8(p), axis=0)
            p_ref[c, rows, :] = p.astype(bf16)
        alpha = jnp.exp(m_old[c] - m_new[c])
        out.append(alpha * l[c] + jnp.sum(l8, axis=0, keepdims=True))
        acc_ref[c] = alpha * acc_ref[c] + _dot(v_t, p_ref[c])
    return out


def _diffattn_body(q_ref, k_ref, v_ref, lamv_ref, subln_ref, const_ref, o_ref,
                   vt_ref, bias_ref, raw_ref, t_ref, p_ref, acc_ref):
    h = pl.program_id(1)
    qi = pl.program_id(2)

    @pl.when(qi == 0)
    def _():
        for j in range(vt_ref.shape[0]):
            vt_ref[j] = v_ref[j * TQ:(j + 1) * TQ, :].astype(f32).T.astype(bf16)

    slope = jnp.where(h == 0, 2.0 ** -2, jnp.where(h == 1, 2.0 ** -4, jnp.where(h == 2, 2.0 ** -6, 2.0 ** -8)))
    slope = slope.astype(f32)
    q_t = (q_ref[...].astype(f32) * (DIFF_DH ** -0.5)).T
    half = lax.broadcasted_iota(i32, (HEAD_W, TQ), 0) < DIFF_DH
    q_cat = jnp.concatenate([jnp.where(half, q_t, 0.0), jnp.where(half, 0.0, q_t)], axis=1).astype(bf16)
    key = lax.broadcasted_iota(i32, (TQ, TQ), 0)
    qry = lax.broadcasted_iota(i32, (TQ, TQ), 1)
    rel = (qry - key).astype(f32)
    bias_ref[0] = -slope * rel
    bias_ref[1] = jnp.where((key // CHUNK) <= (qry // CHUNK), -slope * jnp.abs(rel), NEG_BIG)
    acc_ref[...] = jnp.zeros_like(acc_ref)

    block_off = lambda j: -slope * ((qi - j) * TQ).astype(f32)
    block_bias = lambda j: bias_ref.at[jnp.where(j == qi, 1, 0)]

    neg = jnp.full((1, TQ), NEG_BIG, f32)
    zero = jnp.zeros((1, TQ), f32)
    _raw_scores(k_ref, 0, q_cat, raw_ref)
    m_first = _bias_and_max(raw_ref, block_bias(0), block_off(0), (neg, neg), t_ref.at[0])

    def step(j, carry):
        m_old, m_new, l = carry[0:2], carry[2:4], carry[4:6]
        _raw_scores(k_ref, j + 1, q_cat, raw_ref)
        l = _accumulate(t_ref.at[j % 2], block_off(j), m_old, m_new, l, vt_ref[j], p_ref, acc_ref)
        m_next = _bias_and_max(raw_ref, block_bias(j + 1), block_off(j + 1), m_new, t_ref.at[(j + 1) % 2])
        return (*m_new, *m_next, *l)

    carry = lax.fori_loop(0, qi, step, (neg, neg, *m_first, zero, zero))
    l = _accumulate(t_ref.at[qi % 2], block_off(qi), carry[0:2], carry[2:4], carry[4:6], vt_ref[qi], p_ref, acc_ref)
    outs = [acc_ref[c] / l[c] for c in range(2)]

    lamv = lamv_ref[...]
    lam_init = const_ref[0:1, 0:1]
    lam = (jnp.exp(jnp.sum(lamv[0:1] * lamv[1:2], axis=-1, keepdims=True))
           - jnp.exp(jnp.sum(lamv[2:3] * lamv[3:4], axis=-1, keepdims=True)) + lam_init)
    o = (outs[0] - lam * outs[1]).T
    o = o * lax.rsqrt(jnp.mean(o * o, axis=-1, keepdims=True) + EPS) * subln_ref[...]
    o_ref[...] = (o * (1.0 - lam_init)).astype(bf16)


def _diffattn(proj, lamv, subln, consts, batch, seq):
    nq = seq // TQ
    q0, k0, v0 = (4 * SPLIT_W // HEAD_W, 5 * SPLIT_W // HEAD_W, 6 * SPLIT_W // HEAD_W)
    return pl.pallas_call(
        _diffattn_body,
        out_shape=jax.ShapeDtypeStruct((batch * seq, SPLIT_W), bf16),
        grid=(batch, DIFF_HEADS, nq),
        in_specs=[pl.BlockSpec((TQ, HEAD_W), lambda b, h, i: (b * nq + i, q0 + h)),
                  pl.BlockSpec((seq, HEAD_W), lambda b, h, i: (b, k0 + h)),
                  pl.BlockSpec((seq, HEAD_W), lambda b, h, i: (b, v0 + h)),
                  pl.BlockSpec((8, LANES), lambda b, h, i: (0, 0)),
                  pl.BlockSpec((1, HEAD_W), lambda b, h, i: (0, 0)),
                  pl.BlockSpec((1, LANES), lambda b, h, i: (0, 0))],
        out_specs=pl.BlockSpec((TQ, HEAD_W), lambda b, h, i: (b * nq + i, h)),
        scratch_shapes=[pltpu.VMEM((nq, HEAD_W, TQ), bf16),
                        pltpu.VMEM((2, TQ, TQ), f32),
                        pltpu.VMEM((TQ, 2 * TQ), f32),
                        pltpu.VMEM((2, 2, TQ, TQ), f32),
                        pltpu.VMEM((2, TQ, TQ), bf16),
                        pltpu.VMEM((2, HEAD_W, TQ), f32)],
        compiler_params=_params("parallel", "parallel", "arbitrary"),
        name="diffattn",
    )(proj, proj, proj, lamv, subln, consts)


def _route_rows(lt):
    lg = [lt[r:r + 1, :] for r in range(N_GROUPS)]
    best, gidx = lg[0], jnp.zeros(lg[0].shape, i32)
    for r in range(1, N_GROUPS):
        upd = lg[r] > best
        gidx = jnp.where(upd, r, gidx)
        best = jnp.where(upd, lg[r], best)
    den = sum(jnp.exp(v - best) for v in lg)
    g_w = 1.0 / den

    def sel(e):
        rows = [lt[8 + 4 * g + e:9 + 4 * g + e, :] for g in range(N_GROUPS)]
        return jnp.where(gidx == 0, rows[0], jnp.where(gidx == 1, rows[1], jnp.where(gidx == 2, rows[2], rows[3])))

    le = [sel(e) for e in range(EXPERTS_PER_GROUP)]
    v1, i1 = le[0], jnp.zeros(le[0].shape, i32)
    for e in range(1, EXPERTS_PER_GROUP):
        upd = le[e] > v1
        i1 = jnp.where(upd, e, i1)
        v1 = jnp.where(upd, le[e], v1)
    rest = [jnp.where(i1 == e, -jnp.inf, le[e]) for e in range(EXPERTS_PER_GROUP)]
    v2, i2 = rest[0], jnp.zeros(le[0].shape, i32)
    for e in range(1, EXPERTS_PER_GROUP):
        upd = rest[e] > v2
        i2 = jnp.where(upd, e, i2)
        v2 = jnp.where(upd, rest[e], v2)
    e2 = jnp.exp(v2 - v1)
    w1 = (1.0 / (1.0 + e2)) * g_w
    w2 = (e2 / (1.0 + e2)) * g_w
    first_low = i1 < i2
    lo = jnp.minimum(i1, i2)
    hi = jnp.maximum(i1, i2)
    pair = jnp.where(lo == 0, hi - 1, jnp.where(lo == 1, hi + 1, 5))
    return gidx * N_PAIRS + pair, jnp.where(first_low, w1, w2), jnp.where(first_low, w2, w1)


def _outproj_body(ret_ref, diff_ref, wo_ref, x_ref, g_ref, wr_ref, rb_ref,
                  xw_ref, route_ref, cnt_ref, run_ref):
    @pl.when(pl.program_id(0) == 0)
    def _():
        run_ref[...] = jnp.zeros_like(run_ref)

    x1 = x_ref[...] + _dot(ret_ref[...], wo_ref[:SPLIT_W, :]) + _dot(diff_ref[...], wo_ref[SPLIT_W:, :])
    xw_ref[:, :D_MODEL] = x1

    h = _rms(x1, g_ref[...])
    h_hi = h.astype(bf16)
    h_lo = (h - h_hi.astype(f32)).astype(bf16)
    r1 = _dot(h_hi, wr_ref[...])
    logits = r1[:, :LANES] + (r1[:, LANES:] + _dot(h_lo, wr_ref[:, :LANES])) + rb_ref[...]
    bucket, w_low, w_high = _route_rows(logits.T)

    sub = lax.broadcasted_iota(i32, (LANES, TM), 0)
    wt = jnp.where(sub == 0, w_low, jnp.where(sub == 1, w_high, 0.0))
    xw_ref[:, D_MODEL:] = wt.T

    onehot = lax.broadcasted_iota(i32, (BUCKET_ROWS, TM), 0) == bucket
    oh_b = jnp.where(onehot, 1.0, 0.0).astype(bf16)
    earlier = lax.broadcasted_iota(i32, (TM, TM), 0) < lax.broadcasted_iota(i32, (TM, TM), 1)
    before = _dot(oh_b, jnp.where(earlier, 1.0, 0.0).astype(bf16))
    run = run_ref[...]
    oh_f = jnp.where(onehot, 1.0, 0.0)
    rank = jnp.sum(oh_f * (before + run[:, 0:1]), axis=0, keepdims=True).astype(i32)
    sub8 = lax.broadcasted_iota(i32, (8, TM), 0)
    route_ref[0] = jnp.where(sub8 == 0, bucket, jnp.where(sub8 == 1, rank, 0))
    run = run + jnp.sum(oh_f, axis=1, keepdims=True)
    run_ref[...] = run
    cnt_ref[...] = run


def _outproj(ret, diff, wo, x, g, wr, rb):
    t = x.shape[0]
    nt = t // TM
    return pl.pallas_call(
        _outproj_body,
        out_shape=(jax.ShapeDtypeStruct((t, ROW_W), f32),
                   jax.ShapeDtypeStruct((nt, 8, TM), i32),
                   jax.ShapeDtypeStruct((BUCKET_ROWS, LANES), f32)),
        grid=(nt,),
        in_specs=[pl.BlockSpec((TM, SPLIT_W), lambda i: (i, 0)),
                  pl.BlockSpec((TM, SPLIT_W), lambda i: (i, 0)),
                  pl.BlockSpec((D_MODEL, D_MODEL), lambda i: (0, 0)),
                  pl.BlockSpec((TM, D_MODEL), lambda i: (i, 0)),
                  pl.BlockSpec((1, D_MODEL), lambda i: (0, 0)),
                  pl.BlockSpec((D_MODEL, 2 * LANES), lambda i: (0, 0)),
                  pl.BlockSpec((1, LANES), lambda i: (0, 0))],
        out_specs=(pl.BlockSpec((TM, ROW_W), lambda i: (i, 0)),
                   pl.BlockSpec((1, 8, TM), lambda i: (i, 0, 0)),
                   pl.BlockSpec((BUCKET_ROWS, LANES), lambda i: (0, 0))),
        scratch_shapes=[pltpu.VMEM((BUCKET_ROWS, LANES), f32)],
        compiler_params=_params("arbitrary"),
        name="outproj_router",
    )(ret, diff, wo, x, g, wr, rb)


def _row_copy_out(src_ref, dst_ref, pos_ref, sem, base):
    def copy(r):
        return pltpu.make_async_copy(src_ref.at[pl.ds(r, 1)], dst_ref.at[pl.ds(pos_ref[base + r], 1)], sem)

    def start(r, c):
        copy(r).start()
        return c

    def wait(r, c):
        copy(r).wait()
        return c

    lax.fori_loop(0, TM_PERM, start, 0)
    lax.fori_loop(0, TM_PERM, wait, 0)


def _dispatch_body(pos_ref, x_ref, init_ref, o_ref, sem):
    del init_ref
    _row_copy_out(x_ref, o_ref, pos_ref, sem.at[0], pl.program_id(0) * TM_PERM)


def _dispatch(pos, xw, rows_out):
    t = xw.shape[0]
    return pl.pallas_call(
        _dispatch_body,
        out_shape=jax.ShapeDtypeStruct((rows_out, ROW_W), f32),
        grid_spec=pltpu.PrefetchScalarGridSpec(
            num_scalar_prefetch=1,
            grid=(t // TM_PERM,),
            in_specs=[pl.BlockSpec((TM_PERM, ROW_W), lambda i, pos: (i, 0)),
                      pl.BlockSpec(memory_space=pl.ANY)],
            out_specs=pl.BlockSpec(memory_space=pl.ANY),
            scratch_shapes=[pltpu.SemaphoreType.DMA((1,))]),
        input_output_aliases={2: 0},
        compiler_params=_params("arbitrary"),
        name="dispatch",
    )(pos, xw, jnp.zeros((rows_out, ROW_W), f32))


def _combine_body(pos_ref, src_ref, o_ref, sem):
    base = pl.program_id(0) * TM_PERM

    def copy(r):
        return pltpu.make_async_copy(src_ref.at[pl.ds(pos_ref[base + r], 1)], o_ref.at[pl.ds(r, 1)], sem.at[0])

    def start(r, c):
        copy(r).start()
        return c

    def wait(r, c):
        copy(r).wait()
        return c

    lax.fori_loop(0, TM_PERM, start, 0)
    lax.fori_loop(0, TM_PERM, wait, 0)


def _combine(pos, ys, t):
    return pl.pallas_call(
        _combine_body,
        out_shape=jax.ShapeDtypeStruct((t, D_MODEL), f32),
        grid_spec=pltpu.PrefetchScalarGridSpec(
            num_scalar_prefetch=1,
            grid=(t // TM_PERM,),
            in_specs=[pl.BlockSpec(memory_space=pl.ANY)],
            out_specs=pl.BlockSpec((TM_PERM, D_MODEL), lambda i, pos: (i, 0)),
            scratch_shapes=[pltpu.SemaphoreType.DMA((1,))]),
        compiler_params=_params("arbitrary"),
        name="combine",
    )(pos, ys)


def _moe_body(e_lo_ref, e_hi_ref, valid_ref, xs_ref, g_ref, wg1, wu1, wd1, wg2, wu2, wd2, o_ref):
    nv = valid_ref[pl.program_id(0)]

    @pl.when(nv > 0)
    def _():
        ok = lax.broadcasted_iota(i32, (TM_MOE, 1), 0) < nv
        x = jnp.where(ok, xs_ref[:, :D_MODEL], 0.0)
        w_lo = jnp.where(ok, xs_ref[:, D_MODEL:D_MODEL + 1], 0.0)
        w_hi = jnp.where(ok, xs_ref[:, D_MODEL + 1:D_MODEL + 2], 0.0)
        h = _rms(x, g_ref[...]).astype(bf16)
        hid1 = jax.nn.silu(_dot(h, wg1[0])) * _dot(h, wu1[0]) * w_lo
        hid2 = jax.nn.silu(_dot(h, wg2[0])) * _dot(h, wu2[0]) * w_hi
        o_ref[...] = x + (_dot(hid1.astype(bf16), wd1[0]) + _dot(hid2.astype(bf16), wd2[0]))

    @pl.when(nv == 0)
    def _():
        o_ref[...] = jnp.zeros_like(o_ref)


def _moe(e_lo, e_hi, valid, xs, g, wg, wu, wd):
    n_tiles = xs.shape[0] // TM_MOE
    up = lambda sel: pl.BlockSpec((1, D_MODEL, D_EXPERT), lambda i, lo, hi, nv: ((lo, hi)[sel][i], 0, 0))
    down = lambda sel: pl.BlockSpec((1, D_EXPERT, D_MODEL), lambda i, lo, hi, nv: ((lo, hi)[sel][i], 0, 0))
    return pl.pallas_call(
        _moe_body,
        out_shape=jax.ShapeDtypeStruct((xs.shape[0], D_MODEL), f32),
        grid_spec=pltpu.PrefetchScalarGridSpec(
            num_scalar_prefetch=3,
            grid=(n_tiles,),
            in_specs=[pl.BlockSpec((TM_MOE, ROW_W), lambda i, lo, hi, nv: (i, 0)),
                      pl.BlockSpec((1, D_MODEL), lambda i, lo, hi, nv: (0, 0)),
                      up(0), up(0), down(0), up(1), up(1), down(1)],
            out_specs=pl.BlockSpec((TM_MOE, D_MODEL), lambda i, lo, hi, nv: (i, 0))),
        compiler_params=_params("arbitrary"),
        name="moe",
    )(e_lo, e_hi, valid, xs, g, wg, wu, wd, wg, wu, wd)


def _sort_plan(route, counts_f, n_tiles):
    bucket = route[:, 0, :].reshape(-1)
    rank = route[:, 1, :].reshape(-1)
    counts = counts_f[:N_BUCKETS, 0].astype(i32)
    tiles_per = (counts + TM_MOE - 1) // TM_MOE
    tile_end = jnp.cumsum(tiles_per)
    tile_start = tile_end - tiles_per
    pos = (tile_start * TM_MOE)[bucket] + rank
    tile = jnp.arange(n_tiles, dtype=i32)
    used = tile < tile_end[-1]
    owner = lambda tl: jnp.minimum(jnp.sum((tl[:, None] >= tile_end[None, :]).astype(i32), axis=1), N_BUCKETS - 1)
    tb = owner(jnp.where(used, tile, tile_end[-1] - 1))
    valid = jnp.where(used, jnp.clip(counts[tb] - (tile - tile_start[tb]) * TM_MOE, 0, TM_MOE), 0).astype(i32)
    pair_lo = jnp.array([0, 0, 0, 1, 1, 2], i32)
    pair_hi = jnp.array([1, 2, 3, 2, 3, 3], i32)
    grp = tb // N_PAIRS
    e_lo = grp * EXPERTS_PER_GROUP + pair_lo[tb % N_PAIRS]
    e_hi = grp * EXPERTS_PER_GROUP + pair_hi[tb % N_PAIRS]
    return pos.astype(i32), e_lo, e_hi, valid


def _ple_body(x_ref, p_ref, g_ref, wg_ref, wp_ref, gf_ref, o_ref, *, final):
    x = x_ref[...]
    gate = jax.nn.sigmoid(_dot(_rms(x, g_ref[...]).astype(bf16), wg_ref[...]))
    y = x + gate * _dot(p_ref[...].astype(bf16), wp_ref[...])
    o_ref[...] = _rms(y, gf_ref[...]) if final else y


def _ple(x, p, g, wg, wp, gf, final):
    t = x.shape[0]
    return pl.pallas_call(
        functools.partial(_ple_body, final=final),
        out_shape=jax.ShapeDtypeStruct((t, D_MODEL), f32),
        grid=(t // TM,),
        in_specs=[pl.BlockSpec((TM, D_MODEL), lambda i: (i, 0)),
                  pl.BlockSpec((TM, D_PLE), lambda i: (i, 0)),
                  pl.BlockSpec((1, D_MODEL), lambda i: (0, 0)),
                  pl.BlockSpec((D_MODEL, D_MODEL), lambda i: (0, 0)),
                  pl.BlockSpec((D_PLE, D_MODEL), lambda i: (0, 0)),
                  pl.BlockSpec((1, D_MODEL), lambda i: (0, 0))],
        out_specs=pl.BlockSpec((TM, D_MODEL), lambda i: (i, 0)),
        compiler_params=_params("parallel"),
        name="ple_final" if final else "ple",
    )(x, p, g, wg, wp, gf)


def _router_weights(w_rg, b_rg, w_re, b_re):
    w = jnp.zeros((D_MODEL, LANES), f32)
    w = w.at[:, :N_GROUPS].set(w_rg).at[:, 8:8 + N_EXPERTS].set(w_re)
    hi = w.astype(bf16)
    lo = (w - hi.astype(f32)).astype(bf16)
    b = jnp.zeros((1, LANES), f32).at[0, :N_GROUPS].set(b_rg).at[0, 8:8 + N_EXPERTS].set(b_re)
    return jnp.concatenate([hi, lo], axis=1), b


def kernel(x, p, norm_mix, w_in, lam_q1, lam_k1, lam_q2, lam_k2, diff_subln, w_out, norm_ffn,
           w_router_group, b_router_group, w_router_expert, b_router_expert, w_gate, w_up, w_down,
           norm_ple, w_ple_gate, w_ple_proj, norm_final):
    batch, seq, _ = x.shape
    t = batch * seq
    n_tiles = t // TM_MOE + N_BUCKETS
    xf = x.reshape(t, D_MODEL)
    row = lambda v: v.reshape(1, -1).astype(f32)
    for i in range(DEPTH):
        lam_init = 0.8 - 0.6 * math.exp(-0.3 * i)
        proj = _inproj(xf, row(norm_mix[i]), w_in[i].astype(bf16))
        ret = _retention(proj, batch, seq)
        lamv = jnp.zeros((8, LANES), f32)
        for r, v in enumerate((lam_q1[i], lam_k1[i], lam_q2[i], lam_k2[i])):
            lamv = lamv.at[r, :DIFF_DH].set(v.astype(f32))
        consts = jnp.full((1, LANES), lam_init, f32)
        diff = _diffattn(proj, lamv, row(diff_subln[i]), consts, batch, seq)
        wr, rb = _router_weights(w_router_group[i], b_router_group[i], w_router_expert[i], b_router_expert[i])
        xw, route, counts = _outproj(ret, diff, w_out[i].astype(bf16), xf, row(norm_ffn[i]), wr, rb)
        pos, e_lo, e_hi, valid = _sort_plan(route, counts, n_tiles)
        xs = _dispatch(pos, xw, n_tiles * TM_MOE)
        ys = _moe(e_lo, e_hi, valid, xs, row(norm_ffn[i]),
                  w_gate[i].astype(bf16), w_up[i].astype(bf16), w_down[i].astype(bf16))
        x2 = _combine(pos, ys, t)
        xf = _ple(x2, p[i].reshape(t, D_PLE), row(norm_ple[i]), w_ple_gate[i].astype(bf16),
                  w_ple_proj[i].astype(bf16), row(norm_final), final=(i == DEPTH - 1))
    return xf.reshape(batch, seq, D_MODEL)
```

```python
import functools
import math

import jax
import jax.numpy as jnp
from jax import lax
from jax.experimental import pallas as pl
from jax.experimental.pallas import tpu as pltpu
from jax.experimental.pallas import tpu_sc as plsc

f32 = jnp.float32
bf16 = jnp.bfloat16
i32 = jnp.int32

D_MODEL = 1024
DEPTH = 4
CHUNK = 64
D_PLE = 256
EPS = 1e-6
RET_HEADS = 4
RET_DK = 128
DIFF_HEADS = 4
DIFF_DH = 64
HEAD_W = 128
SPLIT_W = 512
IN_COLS = 7 * SPLIT_W
N_GROUPS = 4
EXPERTS_PER_GROUP = 4
N_EXPERTS = 16
D_EXPERT = 512
N_PAIRS = 6
N_BUCKETS = N_GROUPS * N_PAIRS
BUCKET_ROWS = 32

LANES = 128
X_CHUNKS = D_MODEL // LANES
ROW_CHUNKS = X_CHUNKS + 1

TM = 512
RET_BLOCK = 256
TQ = 256
STRIP = 64
TM_MOE = 512
SC_WINDOW = 128
VMEM_LIMIT = 56 * 1024 * 1024
NEG_BIG = -1e30


def _rms(x, g):
    return x * lax.rsqrt(jnp.mean(x * x, axis=-1, keepdims=True) + EPS) * g


def _dot(a, b):
    return jnp.dot(a, b, preferred_element_type=f32)


def _dot_nt(a, b):
    return lax.dot_general(a, b, (((1,), (1,)), ((), ())), preferred_element_type=f32)


def _dot_tn(a, b):
    return lax.dot_general(a, b, (((0,), (0,)), ((), ())), preferred_element_type=f32)


def _params(*sem):
    return pltpu.CompilerParams(dimension_semantics=sem, vmem_limit_bytes=VMEM_LIMIT)


def _inproj_body(x_ref, g_ref, w_ref, o_ref):
    h = _rms(x_ref[...], g_ref[...]).astype(bf16)
    for n in range(IN_COLS // SPLIT_W):
        cols = slice(n * SPLIT_W, (n + 1) * SPLIT_W)
        o_ref[:, cols] = _dot(h, w_ref[:, cols]).astype(bf16)


def _inproj(x, g, w):
    t = x.shape[0]
    return pl.pallas_call(
        _inproj_body,
        out_shape=jax.ShapeDtypeStruct((t, IN_COLS), bf16),
        grid=(t // TM,),
        in_specs=[pl.BlockSpec((TM, D_MODEL), lambda i: (i, 0)),
                  pl.BlockSpec((1, D_MODEL), lambda i: (0, 0)),
                  pl.BlockSpec((D_MODEL, IN_COLS), lambda i: (0, 0))],
        out_specs=pl.BlockSpec((TM, IN_COLS), lambda i: (i, 0)),
        compiler_params=_params("parallel"),
        name="inproj",
    )(x, g, w)


def _retention_body(q_ref, k_ref, v_ref, g_ref, o_ref, state_ref):
    @pl.when(pl.program_id(1) == 0)
    def _():
        state_ref[...] = jnp.zeros_like(state_ref)

    n = RET_BLOCK
    rel = (lax.broadcasted_iota(i32, (n, n), 0) - lax.broadcasted_iota(i32, (n, n), 1)).astype(f32)
    pos = lax.broadcasted_iota(i32, (n, HEAD_W), 0).astype(f32)
    scale = RET_DK ** -0.5
    for h in range(RET_HEADS):
        log_g = math.log1p(-(2.0 ** (-5.0 - h)))
        cols = slice(h * HEAD_W, (h + 1) * HEAD_W)
        q = q_ref[:, cols]
        k = k_ref[:, cols]
        v = v_ref[:, cols]
        decay = jnp.where(rel >= 0, jnp.exp(log_g * jnp.maximum(rel, 0.0)), 0.0) * scale
        scores = _dot_nt(q, k) * decay
        intra = _dot(scores.astype(bf16), v)
        state = state_ref[h]
        cross = _dot(q, state.astype(bf16)) * jnp.exp(log_g * (pos + 1.0))
        k_dec = (k.astype(f32) * (jnp.exp(log_g * (n - 1.0 - pos)) * scale)).astype(bf16)
        state_ref[h] = math.exp(log_g * n) * state + _dot_tn(k_dec, v)
        o = intra + cross
        o = o * lax.rsqrt(jnp.mean(o * o, axis=-1, keepdims=True) + EPS)
        gate = g_ref[:, cols].astype(f32)
        o_ref[:, cols] = (gate * jax.nn.sigmoid(gate) * o).astype(bf16)


def _retention(proj, batch, seq):
    nb = seq // RET_BLOCK
    spec = lambda c: pl.BlockSpec((RET_BLOCK, SPLIT_W), lambda b, j: (b * nb + j, c))
    return pl.pallas_call(
        _retention_body,
        out_shape=jax.ShapeDtypeStruct((batch * seq, SPLIT_W), bf16),
        grid=(batch, nb),
        in_specs=[spec(0), spec(1), spec(2), spec(3)],
        out_specs=pl.BlockSpec((RET_BLOCK, SPLIT_W), lambda b, j: (b * nb + j, 0)),
        scratch_shapes=[pltpu.VMEM((RET_HEADS, RET_DK, HEAD_W), f32)],
        compiler_params=_params("parallel", "arbitrary"),
        name="retention",
    )(proj, proj, proj, proj)


_STRIPS = [slice(s * STRIP, (s + 1) * STRIP) for s in range(TQ // STRIP)]
_fold8 = lambda a: a.reshape(STRIP // 8, 8, TQ)


def _raw_scores(k_ref, blk, q_cat, raw_ref):
    for rows in _STRIPS:
        kb = k_ref[pl.ds(pl.multiple_of(blk * TQ + rows.start, STRIP), STRIP), :]
        raw_ref[rows, :] = _dot(kb, q_cat)


def _bias_and_max(raw_ref, bias_ref, off, m, t_ref):
    out = []
    for c in range(2):
        m8 = None
        for rows in _STRIPS:
            t = raw_ref[rows, c * TQ:(c + 1) * TQ] + bias_ref[rows, :]
            t_ref[c, rows, :] = t
            g = jnp.max(_fold8(t), axis=0)
            m8 = g if m8 is None else jnp.maximum(m8, g)
        out.append(jnp.maximum(m[c], jnp.max(m8, axis=0, keepdims=True) + off))
    return out


def _accumulate(t_ref, off, m_old, m_new, l, v_t, p_ref, acc_ref):
    out = []
    for c in range(2):
        shift = off - m_new[c]
        l8 = jnp.zeros((8, TQ), f32)
        for rows in _STRIPS:
            p = jnp.exp(t_ref[c, rows, :] + shift)
            l8 = l8 + jnp.sum(_fold8(p), axis=0)
            p_ref[c, rows, :] = p.astype(bf16)
        alpha = jnp.exp(m_old[c] - m_new[c])
        out.append(alpha * l[c] + jnp.sum(l8, axis=0, keepdims=True))
        acc_ref[c] = alpha * acc_ref[c] + _dot(v_t, p_ref[c])
    return out


def _diffattn_body(q_ref, k_ref, v_ref, lamv_ref, subln_ref, const_ref, o_ref,
                   vt_ref, bias_ref, raw_ref, t_ref, p_ref, acc_ref):
    h = pl.program_id(1)
    qi = pl.program_id(2)

    slope = jnp.where(h == 0, 2.0 ** -2, jnp.where(h == 1, 2.0 ** -4, jnp.where(h == 2, 2.0 ** -6, 2.0 ** -8)))
    slope = slope.astype(f32)

    @pl.when(qi == 0)
    def _():
        for j in range(vt_ref.shape[0]):
            vt_ref[j] = v_ref[j * TQ:(j + 1) * TQ, :].astype(f32).T.astype(bf16)
        key = lax.broadcasted_iota(i32, (TQ, TQ), 0)
        qry = lax.broadcasted_iota(i32, (TQ, TQ), 1)
        rel = (qry - key).astype(f32)
        bias_ref[0] = -slope * rel
        bias_ref[1] = jnp.where((key // CHUNK) <= (qry // CHUNK), -slope * jnp.abs(rel), NEG_BIG)

    q_t = (q_ref[...].astype(f32) * (DIFF_DH ** -0.5)).T
    half = lax.broadcasted_iota(i32, (HEAD_W, TQ), 0) < DIFF_DH
    q_cat = jnp.concatenate([jnp.where(half, q_t, 0.0), jnp.where(half, 0.0, q_t)], axis=1).astype(bf16)
    acc_ref[...] = jnp.zeros_like(acc_ref)

    block_off = lambda j: -slope * ((qi - j) * TQ).astype(f32)
    block_bias = lambda j: bias_ref.at[jnp.where(j == qi, 1, 0)]

    neg = jnp.full((1, TQ), NEG_BIG, f32)
    zero = jnp.zeros((1, TQ), f32)
    _raw_scores(k_ref, 0, q_cat, raw_ref)
    m_first = _bias_and_max(raw_ref, block_bias(0), block_off(0), (neg, neg), t_ref.at[0])

    def step(j, carry):
        m_old, m_new, l = carry[0:2], carry[2:4], carry[4:6]
        _raw_scores(k_ref, j + 1, q_cat, raw_ref)
        l = _accumulate(t_ref.at[j % 2], block_off(j), m_old, m_new, l, vt_ref[j], p_ref, acc_ref)
        m_next = _bias_and_max(raw_ref, block_bias(j + 1), block_off(j + 1), m_new, t_ref.at[(j + 1) % 2])
        return (*m_new, *m_next, *l)

    carry = lax.fori_loop(0, qi, step, (neg, neg, *m_first, zero, zero))
    l = _accumulate(t_ref.at[qi % 2], block_off(qi), carry[0:2], carry[2:4], carry[4:6], vt_ref[qi], p_ref, acc_ref)
    outs = [acc_ref[c] / l[c] for c in range(2)]

    lamv = lamv_ref[...]
    lam_init = const_ref[0:1, 0:1]
    lam = (jnp.exp(jnp.sum(lamv[0:1] * lamv[1:2], axis=-1, keepdims=True))
           - jnp.exp(jnp.sum(lamv[2:3] * lamv[3:4], axis=-1, keepdims=True)) + lam_init)
    o = (outs[0] - lam * outs[1]).T
    o = o * lax.rsqrt(jnp.mean(o * o, axis=-1, keepdims=True) + EPS) * subln_ref[...]
    o_ref[...] = (o * (1.0 - lam_init)).astype(bf16)


def _diffattn(proj, lamv, subln, consts, batch, seq):
    nq = seq // TQ
    q0, k0, v0 = (4 * SPLIT_W // HEAD_W, 5 * SPLIT_W // HEAD_W, 6 * SPLIT_W // HEAD_W)
    return pl.pallas_call(
        _diffattn_body,
        out_shape=jax.ShapeDtypeStruct((batch * seq, SPLIT_W), bf16),
        grid=(batch, DIFF_HEADS, nq),
        in_specs=[pl.BlockSpec((TQ, HEAD_W), lambda b, h, i: (b * nq + i, q0 + h)),
                  pl.BlockSpec((seq, HEAD_W), lambda b, h, i: (b, k0 + h)),
                  pl.BlockSpec((seq, HEAD_W), lambda b, h, i: (b, v0 + h)),
                  pl.BlockSpec((8, LANES), lambda b, h, i: (0, 0)),
                  pl.BlockSpec((1, HEAD_W), lambda b, h, i: (0, 0)),
                  pl.BlockSpec((1, LANES), lambda b, h, i: (0, 0))],
        out_specs=pl.BlockSpec((TQ, HEAD_W), lambda b, h, i: (b * nq + i, h)),
        scratch_shapes=[pltpu.VMEM((nq, HEAD_W, TQ), bf16),
                        pltpu.VMEM((2, TQ, TQ), f32),
                        pltpu.VMEM((TQ, 2 * TQ), f32),
                        pltpu.VMEM((2, 2, TQ, TQ), f32),
                        pltpu.VMEM((2, TQ, TQ), bf16),
                        pltpu.VMEM((2, HEAD_W, TQ), f32)],
        compiler_params=_params("parallel", "parallel", "arbitrary"),
        name="diffattn",
    )(proj, proj, proj, lamv, subln, consts)


def _route_rows(lt):
    lg = [lt[r:r + 1, :] for r in range(N_GROUPS)]
    best, gidx = lg[0], jnp.zeros(lg[0].shape, i32)
    for r in range(1, N_GROUPS):
        upd = lg[r] > best
        gidx = jnp.where(upd, r, gidx)
        best = jnp.where(upd, lg[r], best)
    den = sum(jnp.exp(v - best) for v in lg)
    g_w = 1.0 / den

    def sel(e):
        rows = [lt[8 + 4 * g + e:9 + 4 * g + e, :] for g in range(N_GROUPS)]
        return jnp.where(gidx == 0, rows[0], jnp.where(gidx == 1, rows[1], jnp.where(gidx == 2, rows[2], rows[3])))

    le = [sel(e) for e in range(EXPERTS_PER_GROUP)]
    v1, i1 = le[0], jnp.zeros(le[0].shape, i32)
    for e in range(1, EXPERTS_PER_GROUP):
        upd = le[e] > v1
        i1 = jnp.where(upd, e, i1)
        v1 = jnp.where(upd, le[e], v1)
    rest = [jnp.where(i1 == e, -jnp.inf, le[e]) for e in range(EXPERTS_PER_GROUP)]
    v2, i2 = rest[0], jnp.zeros(le[0].shape, i32)
    for e in range(1, EXPERTS_PER_GROUP):
        upd = rest[e] > v2
        i2 = jnp.where(upd, e, i2)
        v2 = jnp.where(upd, rest[e], v2)
    e2 = jnp.exp(v2 - v1)
    w1 = (1.0 / (1.0 + e2)) * g_w
    w2 = (e2 / (1.0 + e2)) * g_w
    first_low = i1 < i2
    lo = jnp.minimum(i1, i2)
    hi = jnp.maximum(i1, i2)
    pair = jnp.where(lo == 0, hi - 1, jnp.where(lo == 1, hi + 1, 5))
    return gidx * N_PAIRS + pair, jnp.where(first_low, w1, w2), jnp.where(first_low, w2, w1)


def _outproj_body(ret_ref, diff_ref, wo_ref, x_ref, g_ref, wr_ref, rb_ref,
                  xw_ref, route_ref, cnt_ref, run_ref):
    @pl.when(pl.program_id(0) == 0)
    def _():
        run_ref[...] = jnp.zeros_like(run_ref)

    x1 = x_ref[...] + _dot(ret_ref[...], wo_ref[:SPLIT_W, :]) + _dot(diff_ref[...], wo_ref[SPLIT_W:, :])
    for c in range(X_CHUNKS):
        xw_ref[c] = x1[:, c * LANES:(c + 1) * LANES]

    h = _rms(x1, g_ref[...])
    h_hi = h.astype(bf16)
    h_lo = (h - h_hi.astype(f32)).astype(bf16)
    r1 = _dot(h_hi, wr_ref[...])
    logits = r1[:, :LANES] + (r1[:, LANES:] + _dot(h_lo, wr_ref[:, :LANES])) + rb_ref[...]
    bucket, w_low, w_high = _route_rows(logits.T)

    sub = lax.broadcasted_iota(i32, (LANES, TM), 0)
    wt = jnp.where(sub == 0, w_low, jnp.where(sub == 1, w_high, 0.0))
    xw_ref[X_CHUNKS] = wt.T

    onehot = lax.broadcasted_iota(i32, (BUCKET_ROWS, TM), 0) == bucket
    oh_b = jnp.where(onehot, 1.0, 0.0).astype(bf16)
    earlier = lax.broadcasted_iota(i32, (TM, TM), 0) < lax.broadcasted_iota(i32, (TM, TM), 1)
    before = _dot(oh_b, jnp.where(earlier, 1.0, 0.0).astype(bf16))
    run = run_ref[...]
    oh_f = jnp.where(onehot, 1.0, 0.0)
    rank = jnp.sum(oh_f * (before + run[:, 0:1]), axis=0, keepdims=True).astype(i32)
    sub8 = lax.broadcasted_iota(i32, (8, TM), 0)
    route_ref[0] = jnp.where(sub8 == 0, bucket, jnp.where(sub8 == 1, rank, 0))
    run = run + jnp.sum(oh_f, axis=1, keepdims=True)
    run_ref[...] = run
    cnt_ref[...] = run


def _outproj(ret, diff, wo, x, g, wr, rb):
    t = x.shape[0]
    nt = t // TM
    return pl.pallas_call(
        _outproj_body,
        out_shape=(jax.ShapeDtypeStruct((ROW_CHUNKS, t, LANES), f32),
                   jax.ShapeDtypeStruct((nt, 8, TM), i32),
                   jax.ShapeDtypeStruct((BUCKET_ROWS, LANES), f32)),
        grid=(nt,),
        in_specs=[pl.BlockSpec((TM, SPLIT_W), lambda i: (i, 0)),
                  pl.BlockSpec((TM, SPLIT_W), lambda i: (i, 0)),
                  pl.BlockSpec((D_MODEL, D_MODEL), lambda i: (0, 0)),
                  pl.BlockSpec((TM, D_MODEL), lambda i: (i, 0)),
                  pl.BlockSpec((1, D_MODEL), lambda i: (0, 0)),
                  pl.BlockSpec((D_MODEL, 2 * LANES), lambda i: (0, 0)),
                  pl.BlockSpec((1, LANES), lambda i: (0, 0))],
        out_specs=(pl.BlockSpec((ROW_CHUNKS, TM, LANES), lambda i: (0, i, 0)),
                   pl.BlockSpec((1, 8, TM), lambda i: (i, 0, 0)),
                   pl.BlockSpec((BUCKET_ROWS, LANES), lambda i: (0, 0))),
        scratch_shapes=[pltpu.VMEM((BUCKET_ROWS, LANES), f32)],
        compiler_params=_params("arbitrary"),
        name="outproj_router",
    )(ret, diff, wo, x, g, wr, rb)


def _sc_mesh():
    return plsc.VectorSubcoreMesh(core_axis_name="core", subcore_axis_name="subcore")


def _chunk_rows(pos, chunks, rows_per_chunk):
    return (pos[None, :] + jnp.arange(chunks, dtype=i32)[:, None] * rows_per_chunk).reshape(1, -1)


def _sc_dispatch(pos, xw, rows_out):
    chunks, t, _ = xw.shape
    n = chunks * t

    @functools.partial(pl.kernel, out_type=jax.ShapeDtypeStruct((chunks * rows_out, LANES), xw.dtype),
                       mesh=_sc_mesh(), scratch_types=[])
    def run(x_hbm, i_hbm, o_hbm):
        def body(x_vmem, i_vmem):
            pltpu.sync_copy(x_vmem, o_hbm.at[i_vmem.at[0]])

        pltpu.emit_pipeline(
            body,
            grid=(n // SC_WINDOW,),
            in_specs=[pl.BlockSpec((SC_WINDOW, LANES), lambda i: (i, 0)),
                      pl.BlockSpec((1, SC_WINDOW), lambda i: (0, i))],
            out_specs=[],
            core_axis_name=("core", "subcore"),
            dimension_semantics=(pltpu.PARALLEL,),
        )(x_hbm, i_hbm)

    out = run(xw.reshape(n, LANES), _chunk_rows(pos, chunks, rows_out))
    return out.reshape(chunks, rows_out, LANES)


def _sc_combine(pos, ys, t):
    chunks, rows_in, _ = ys.shape
    n = chunks * t

    @functools.partial(pl.kernel, out_type=jax.ShapeDtypeStruct((n, LANES), ys.dtype),
                       mesh=_sc_mesh(), scratch_types=[])
    def run(y_hbm, i_hbm, o_hbm):
        def body(i_vmem, o_vmem):
            pltpu.sync_copy(y_hbm.at[i_vmem.at[0]], o_vmem)

        pltpu.emit_pipeline(
            body,
            grid=(n // SC_WINDOW,),
            in_specs=[pl.BlockSpec((1, SC_WINDOW), lambda i: (0, i))],
            out_specs=[pl.BlockSpec((SC_WINDOW, LANES), lambda i: (i, 0))],
            core_axis_name=("core", "subcore"),
            dimension_semantics=(pltpu.PARALLEL,),
        )(i_hbm, o_hbm)

    out = run(ys.reshape(chunks * rows_in, LANES), _chunk_rows(pos, chunks, rows_in))
    return out.reshape(chunks, t, LANES)


def _moe_body(e_lo_ref, e_hi_ref, valid_ref, xs_ref, g_ref, wg1, wu1, wd1, wg2, wu2, wd2, o_ref):
    nv = valid_ref[pl.program_id(0)]

    @pl.when(nv > 0)
    def _():
        ok = lax.broadcasted_iota(i32, (TM_MOE, 1), 0) < nv
        x = jnp.where(ok, jnp.concatenate([xs_ref[c] for c in range(X_CHUNKS)], axis=1), 0.0)
        w_lo = jnp.where(ok, xs_ref[X_CHUNKS, :, 0:1], 0.0)
        w_hi = jnp.where(ok, xs_ref[X_CHUNKS, :, 1:2], 0.0)
        h = _rms(x, g_ref[...]).astype(bf16)
        cast = lambda w_ref: w_ref[...].astype(bf16)
        hid1 = jax.nn.silu(_dot(h, cast(wg1))) * _dot(h, cast(wu1)) * w_lo
        hid2 = jax.nn.silu(_dot(h, cast(wg2))) * _dot(h, cast(wu2)) * w_hi
        y = x + (_dot(hid1.astype(bf16), cast(wd1)) + _dot(hid2.astype(bf16), cast(wd2)))
        for c in range(X_CHUNKS):
            o_ref[c] = y[:, c * LANES:(c + 1) * LANES]

    @pl.when(nv == 0)
    def _():
        o_ref[...] = jnp.zeros_like(o_ref)


def _moe(layer, e_lo, e_hi, valid, xs, g, wg, wu, wd):
    n_tiles = xs.shape[1] // TM_MOE
    up = lambda sel: pl.BlockSpec((None, None, D_MODEL, D_EXPERT),
                                  lambda i, lo, hi, nv: (layer, (lo, hi)[sel][i], 0, 0))
    down = lambda sel: pl.BlockSpec((None, None, D_EXPERT, D_MODEL),
                                    lambda i, lo, hi, nv: (layer, (lo, hi)[sel][i], 0, 0))
    return pl.pallas_call(
        _moe_body,
        out_shape=jax.ShapeDtypeStruct((X_CHUNKS, xs.shape[1], LANES), f32),
        grid_spec=pltpu.PrefetchScalarGridSpec(
            num_scalar_prefetch=3,
            grid=(n_tiles,),
            in_specs=[pl.BlockSpec((ROW_CHUNKS, TM_MOE, LANES), lambda i, lo, hi, nv: (0, i, 0)),
                      pl.BlockSpec((1, D_MODEL), lambda i, lo, hi, nv: (0, 0)),
                      up(0), up(0), down(0), up(1), up(1), down(1)],
            out_specs=pl.BlockSpec((X_CHUNKS, TM_MOE, LANES), lambda i, lo, hi, nv: (0, i, 0))),
        compiler_params=_params("arbitrary"),
        name="moe",
    )(e_lo, e_hi, valid, xs, g, wg, wu, wd, wg, wu, wd)


def _sort_plan(route, counts_f, n_tiles):
    bucket = route[:, 0, :].reshape(-1)
    rank = route[:, 1, :].reshape(-1)
    counts = counts_f[:N_BUCKETS, 0].astype(i32)
    tiles_per = (counts + TM_MOE - 1) // TM_MOE
    tile_end = jnp.cumsum(tiles_per)
    tile_start = tile_end - tiles_per
    pos = (tile_start * TM_MOE)[bucket] + rank
    tile = jnp.arange(n_tiles, dtype=i32)
    used = tile < tile_end[-1]
    owner = lambda tl: jnp.minimum(jnp.sum((tl[:, None] >= tile_end[None, :]).astype(i32), axis=1), N_BUCKETS - 1)
    tb = owner(jnp.where(used, tile, tile_end[-1] - 1))
    valid = jnp.where(used, jnp.clip(counts[tb] - (tile - tile_start[tb]) * TM_MOE, 0, TM_MOE), 0).astype(i32)
    pair_lo = jnp.array([0, 0, 0, 1, 1, 2], i32)
    pair_hi = jnp.array([1, 2, 3, 2, 3, 3], i32)
    grp = tb // N_PAIRS
    e_lo = grp * EXPERTS_PER_GROUP + pair_lo[tb % N_PAIRS]
    e_hi = grp * EXPERTS_PER_GROUP + pair_hi[tb % N_PAIRS]
    return pos.astype(i32), e_lo, e_hi, valid


def _ple_body(x_ref, p_ref, g_ref, wg_ref, wp_ref, gf_ref, o_ref, *, final):
    x = jnp.concatenate([x_ref[c] for c in range(X_CHUNKS)], axis=1)
    gate = jax.nn.sigmoid(_dot(_rms(x, g_ref[...]).astype(bf16), wg_ref[...]))
    y = x + gate * _dot(p_ref[...].astype(bf16), wp_ref[...])
    o_ref[...] = _rms(y, gf_ref[...]) if final else y


def _ple(layer, x, p, g, wg, wp, gf, final):
    t = x.shape[1]
    nt = t // TM
    return pl.pallas_call(
        functools.partial(_ple_body, final=final),
        out_shape=jax.ShapeDtypeStruct((t, D_MODEL), f32),
        grid=(nt,),
        in_specs=[pl.BlockSpec((X_CHUNKS, TM, LANES), lambda i: (0, i, 0)),
                  pl.BlockSpec((TM, D_PLE), lambda i: (layer * nt + i, 0)),
                  pl.BlockSpec((1, D_MODEL), lambda i: (0, 0)),
                  pl.BlockSpec((D_MODEL, D_MODEL), lambda i: (0, 0)),
                  pl.BlockSpec((D_PLE, D_MODEL), lambda i: (0, 0)),
                  pl.BlockSpec((1, D_MODEL), lambda i: (0, 0))],
        out_specs=pl.BlockSpec((TM, D_MODEL), lambda i: (i, 0)),
        compiler_params=_params("parallel"),
        name="ple_final" if final else "ple",
    )(x, p, g, wg, wp, gf)


def _router_weights(w_rg, b_rg, w_re, b_re):
    w = jnp.zeros((D_MODEL, LANES), f32)
    w = w.at[:, :N_GROUPS].set(w_rg).at[:, 8:8 + N_EXPERTS].set(w_re)
    hi = w.astype(bf16)
    lo = (w - hi.astype(f32)).astype(bf16)
    b = jnp.zeros((1, LANES), f32).at[0, :N_GROUPS].set(b_rg).at[0, 8:8 + N_EXPERTS].set(b_re)
    return jnp.concatenate([hi, lo], axis=1), b


def kernel(x, p, norm_mix, w_in, lam_q1, lam_k1, lam_q2, lam_k2, diff_subln, w_out, norm_ffn,
           w_router_group, b_router_group, w_router_expert, b_router_expert, w_gate, w_up, w_down,
           norm_ple, w_ple_gate, w_ple_proj, norm_final):
    batch, seq, _ = x.shape
    t = batch * seq
    n_tiles = t // TM_MOE + N_BUCKETS
    xf = x.reshape(t, D_MODEL)
    p_rows = p.reshape(DEPTH * t, D_PLE)
    row = lambda v: v.reshape(1, -1).astype(f32)
    for i in range(DEPTH):
        lam_init = 0.8 - 0.6 * math.exp(-0.3 * i)
        proj = _inproj(xf, row(norm_mix[i]), w_in[i].astype(bf16))
        ret = _retention(proj, batch, seq)
        lamv = jnp.zeros((8, LANES), f32)
        for r, v in enumerate((lam_q1[i], lam_k1[i], lam_q2[i], lam_k2[i])):
            lamv = lamv.at[r, :DIFF_DH].set(v.astype(f32))
        consts = jnp.full((1, LANES), lam_init, f32)
        diff = _diffattn(proj, lamv, row(diff_subln[i]), consts, batch, seq)
        wr, rb = _router_weights(w_router_group[i], b_router_group[i], w_router_expert[i], b_router_expert[i])
        xw, route, counts = _outproj(ret, diff, w_out[i].astype(bf16), xf, row(norm_ffn[i]), wr, rb)
        pos, e_lo, e_hi, valid = _sort_plan(route, counts, n_tiles)
        xs = _sc_dispatch(pos, xw, n_tiles * TM_MOE)
        ys = _moe(i, e_lo, e_hi, valid, xs, row(norm_ffn[i]), w_gate, w_up, w_down)
        x2 = _sc_combine(pos, ys, t)
        xf = _ple(i, x2, p_rows, row(norm_ple[i]), w_ple_gate[i].astype(bf16),
                  w_ple_proj[i].astype(bf16), row(norm_final), final=(i == DEPTH - 1))
    return xf.reshape(batch, seq, D_MODEL)
```

```python
import functools
import math

import jax
import jax.numpy as jnp
from jax import lax
from jax.experimental import pallas as pl
from jax.experimental.pallas import tpu as pltpu
from jax.experimental.pallas import tpu_sc as plsc

f32 = jnp.float32
bf16 = jnp.bfloat16
i32 = jnp.int32

D_MODEL = 1024
DEPTH = 4
CHUNK = 64
D_PLE = 256
EPS = 1e-6
RET_HEADS = 4
RET_DK = 128
DIFF_HEADS = 4
DIFF_DH = 64
HEAD_W = 128
SPLIT_W = 512
IN_COLS = 7 * SPLIT_W
N_GROUPS = 4
EXPERTS_PER_GROUP = 4
N_EXPERTS = 16
D_EXPERT = 512
N_PAIRS = 6
N_BUCKETS = N_GROUPS * N_PAIRS
BUCKET_ROWS = 32

LANES = 128
X_CHUNKS = D_MODEL // LANES
ROW_CHUNKS = X_CHUNKS + 1

TM = 512
RET_BLOCK = 256
TQ = 256
STRIP = 64
TM_MOE = 512
SC_WINDOW = 128
VMEM_LIMIT = 56 * 1024 * 1024
NEG_BIG = -1e30


def _rms(x, g):
    return x * lax.rsqrt(jnp.mean(x * x, axis=-1, keepdims=True) + EPS) * g


def _dot(a, b):
    return jnp.dot(a, b, preferred_element_type=f32)


def _dot_nt(a, b):
    return lax.dot_general(a, b, (((1,), (1,)), ((), ())), preferred_element_type=f32)


def _dot_tn(a, b):
    return lax.dot_general(a, b, (((0,), (0,)), ((), ())), preferred_element_type=f32)


def _params(*sem):
    return pltpu.CompilerParams(dimension_semantics=sem, vmem_limit_bytes=VMEM_LIMIT)


def _inproj_body(x_ref, g_ref, w_ref, o_ref):
    h = _rms(x_ref[...], g_ref[...]).astype(bf16)
    for n in range(IN_COLS // SPLIT_W):
        cols = slice(n * SPLIT_W, (n + 1) * SPLIT_W)
        o_ref[:, cols] = _dot(h, w_ref[:, cols]).astype(bf16)


def _inproj(x, g, w):
    t = x.shape[0]
    return pl.pallas_call(
        _inproj_body,
        out_shape=jax.ShapeDtypeStruct((t, IN_COLS), bf16),
        grid=(t // TM,),
        in_specs=[pl.BlockSpec((TM, D_MODEL), lambda i: (i, 0)),
                  pl.BlockSpec((1, D_MODEL), lambda i: (0, 0)),
                  pl.BlockSpec((D_MODEL, IN_COLS), lambda i: (0, 0))],
        out_specs=pl.BlockSpec((TM, IN_COLS), lambda i: (i, 0)),
        compiler_params=_params("parallel"),
        name="inproj",
    )(x, g, w)


def _retention_body(q_ref, k_ref, v_ref, g_ref, o_ref, state_ref, decay_ref, qdec_ref, kdec_ref):
    n = RET_BLOCK
    scale = RET_DK ** -0.5
    log_gamma = [math.log1p(-(2.0 ** (-5.0 - h))) for h in range(RET_HEADS)]

    @pl.when(pl.program_id(1) == 0)
    def _():
        state_ref[...] = jnp.zeros_like(state_ref)
        rel = (lax.broadcasted_iota(i32, (n, n), 0) - lax.broadcasted_iota(i32, (n, n), 1)).astype(f32)
        pos = lax.broadcasted_iota(i32, (n, HEAD_W), 0).astype(f32)
        for h, log_g in enumerate(log_gamma):
            decay_ref[h] = jnp.where(rel >= 0, jnp.exp(log_g * jnp.maximum(rel, 0.0)), 0.0) * scale
            qdec_ref[h] = jnp.exp(log_g * (pos + 1.0))
            kdec_ref[h] = jnp.exp(log_g * (n - 1.0 - pos)) * scale

    for h, log_g in enumerate(log_gamma):
        cols = slice(h * HEAD_W, (h + 1) * HEAD_W)
        q = q_ref[:, cols]
        k = k_ref[:, cols]
        v = v_ref[:, cols]
        scores = _dot_nt(q, k) * decay_ref[h]
        intra = _dot(scores.astype(bf16), v)
        state = state_ref[h]
        cross = _dot(q, state.astype(bf16)) * qdec_ref[h]
        k_dec = (k.astype(f32) * kdec_ref[h]).astype(bf16)
        state_ref[h] = math.exp(log_g * n) * state + _dot_tn(k_dec, v)
        o = intra + cross
        o = o * lax.rsqrt(jnp.mean(o * o, axis=-1, keepdims=True) + EPS)
        gate = g_ref[:, cols].astype(f32)
        o_ref[:, cols] = (gate * jax.nn.sigmoid(gate) * o).astype(bf16)


def _retention(proj, batch, seq):
    nb = seq // RET_BLOCK
    spec = lambda c: pl.BlockSpec((RET_BLOCK, SPLIT_W), lambda b, j: (b * nb + j, c))
    return pl.pallas_call(
        _retention_body,
        out_shape=jax.ShapeDtypeStruct((batch * seq, SPLIT_W), bf16),
        grid=(batch, nb),
        in_specs=[spec(0), spec(1), spec(2), spec(3)],
        out_specs=pl.BlockSpec((RET_BLOCK, SPLIT_W), lambda b, j: (b * nb + j, 0)),
        scratch_shapes=[pltpu.VMEM((RET_HEADS, RET_DK, HEAD_W), f32),
                        pltpu.VMEM((RET_HEADS, RET_BLOCK, RET_BLOCK), f32),
                        pltpu.VMEM((RET_HEADS, RET_BLOCK, HEAD_W), f32),
                        pltpu.VMEM((RET_HEADS, RET_BLOCK, HEAD_W), f32)],
        compiler_params=_params("parallel", "arbitrary"),
        name="retention",
    )(proj, proj, proj, proj)


_STRIPS = [slice(s * STRIP, (s + 1) * STRIP) for s in range(TQ // STRIP)]
_fold8 = lambda a: a.reshape(STRIP // 8, 8, TQ)


N_MAPS = 2 * DIFF_HEADS
_SLOPES = [2.0 ** (-8.0 * (h + 1) / DIFF_HEADS) for h in range(DIFF_HEADS)]
_M_OLD, _M_NEW = 0, 1
V_ROWS = HEAD_W + 16
K_AUG = 2 * HEAD_W


def _raw_scores(k_ref, feat_ref, blk, qa_ref, raw_ref):
    feat = feat_ref[...]
    for h in range(DIFF_HEADS):
        kb = k_ref[pl.ds(pl.multiple_of(blk * TQ, TQ), TQ), h * HEAD_W:(h + 1) * HEAD_W]
        raw_ref[h] = _dot(jnp.concatenate([kb, feat], axis=1), qa_ref[h])


def _advance_max(raw_ref, bias_ref, dist, m_ref):
    for h in range(DIFF_HEADS):
        for c in range(2):
            m8 = None
            for rows in _STRIPS:
                t = raw_ref[h, rows, c * TQ:(c + 1) * TQ]
                if bias_ref is not None:
                    t = t + bias_ref[h, rows, :]
                g = jnp.max(_fold8(t), axis=0)
                m8 = g if m8 is None else jnp.maximum(m8, g)
            prev = m_ref[_M_NEW, 2 * h + c]
            m_ref[_M_OLD, 2 * h + c] = prev
            m_ref[_M_NEW, 2 * h + c] = jnp.maximum(prev, jnp.max(m8, axis=0, keepdims=True) - _SLOPES[h] * dist)


def _accumulate(raw_ref, bias_ref, dist, m_ref, vt_ref, blk, acc_ref):
    for h in range(DIFF_HEADS):
        for c in range(2):
            m = 2 * h + c
            m_old, m_new = m_ref[_M_OLD, m], m_ref[_M_NEW, m]
            shift = -_SLOPES[h] * dist - m_new
            p = []
            for rows in _STRIPS:
                t = raw_ref[h, rows, c * TQ:(c + 1) * TQ]
                if bias_ref is not None:
                    t = t + bias_ref[h, rows, :]
                p.append(jnp.exp(t + shift).astype(bf16))
            pv = _dot(vt_ref[h, blk], jnp.concatenate(p, axis=0))
            acc_ref[m] = jnp.exp(m_old - m_new) * acc_ref[m] + pv


def _diffattn_body(q_ref, k_ref, v_ref, lamv_ref, subln_ref, const_ref, o_ref,
                   vt_ref, bias_ref, feat_ref, qa_ref, raw_ref, acc_ref, m_ref):
    qi = pl.program_id(1)

    @pl.when(qi == 0)
    def _():
        key = lax.broadcasted_iota(i32, (TQ, TQ), 0)
        qry = lax.broadcasted_iota(i32, (TQ, TQ), 1)
        dist_in = jnp.abs(qry - key).astype(f32)
        allowed = (key // CHUNK) <= (qry // CHUNK)
        ones_rows = jnp.where(lax.broadcasted_iota(i32, (V_ROWS - HEAD_W, TQ), 0) == 0, 1.0, 0.0).astype(bf16)
        sub = lax.broadcasted_iota(i32, (HEAD_W, 2 * TQ), 0)
        qpos = (lax.broadcasted_iota(i32, (HEAD_W, 2 * TQ), 1) % TQ).astype(f32)
        for h in range(DIFF_HEADS):
            cols = slice(h * HEAD_W, (h + 1) * HEAD_W)
            for j in range(vt_ref.shape[1]):
                vt_ref[h, j, :HEAD_W, :] = v_ref[j * TQ:(j + 1) * TQ, cols].astype(f32).T.astype(bf16)
                vt_ref[h, j, HEAD_W:, :] = ones_rows
            bias_ref[h] = jnp.where(allowed, -_SLOPES[h] * dist_in, NEG_BIG)
            qa_ref[h, HEAD_W:, :] = jnp.where(sub == 0, _SLOPES[h],
                                              jnp.where(sub == 1, -_SLOPES[h] * qpos, 0.0)).astype(bf16)
        lane = lax.broadcasted_iota(i32, (TQ, HEAD_W), 1)
        krow = lax.broadcasted_iota(i32, (TQ, HEAD_W), 0).astype(f32)
        feat_ref[0] = jnp.where(lane == 0, krow, jnp.where(lane == 1, 1.0, 0.0)).astype(bf16)
        feat_ref[1] = jnp.zeros((TQ, HEAD_W), bf16)

    half = lax.broadcasted_iota(i32, (HEAD_W, TQ), 0) < DIFF_DH
    for h in range(DIFF_HEADS):
        q_t = (q_ref[:, h * HEAD_W:(h + 1) * HEAD_W].astype(f32) * (DIFF_DH ** -0.5)).T
        qa_ref[h, :HEAD_W, :] = jnp.concatenate([jnp.where(half, q_t, 0.0), jnp.where(half, 0.0, q_t)],
                                                axis=1).astype(bf16)
    acc_ref[...] = jnp.zeros_like(acc_ref)
    m_ref[_M_NEW] = jnp.full(m_ref.shape[1:], NEG_BIG, f32)

    dist = lambda j: ((qi - j) * TQ).astype(f32)
    zero = jnp.zeros((), f32)
    before, diagonal = feat_ref.at[0], feat_ref.at[1]

    @pl.when(qi == 0)
    def _():
        _raw_scores(k_ref, diagonal, 0, qa_ref, raw_ref.at[0])
        _advance_max(raw_ref.at[0], bias_ref, zero, m_ref)

    @pl.when(qi > 0)
    def _():
        _raw_scores(k_ref, before, 0, qa_ref, raw_ref.at[0])
        _advance_max(raw_ref.at[0], None, dist(0), m_ref)

    def step(j, carry):
        _raw_scores(k_ref, before, j + 1, qa_ref, raw_ref.at[(j + 1) % 2])
        _accumulate(raw_ref.at[j % 2], None, dist(j), m_ref, vt_ref, j, acc_ref)
        _advance_max(raw_ref.at[(j + 1) % 2], None, dist(j + 1), m_ref)
        return carry

    lax.fori_loop(0, qi - 1, step, 0)

    @pl.when(qi > 0)
    def _():
        _raw_scores(k_ref, diagonal, qi, qa_ref, raw_ref.at[qi % 2])
        _accumulate(raw_ref.at[(qi - 1) % 2], None, dist(qi - 1), m_ref, vt_ref, qi - 1, acc_ref)
        _advance_max(raw_ref.at[qi % 2], bias_ref, zero, m_ref)

    _accumulate(raw_ref.at[qi % 2], bias_ref, zero, m_ref, vt_ref, qi, acc_ref)

    lamv = lamv_ref[...]
    lam_init = const_ref[0:1, 0:1]
    lam = (jnp.exp(jnp.sum(lamv[0:1] * lamv[1:2], axis=-1, keepdims=True))
           - jnp.exp(jnp.sum(lamv[2:3] * lamv[3:4], axis=-1, keepdims=True)) + lam_init)
    for h in range(DIFF_HEADS):
        a1, a2 = (acc_ref[2 * h + c, :HEAD_W, :] / acc_ref[2 * h + c, HEAD_W:HEAD_W + 1, :] for c in range(2))
        o = (a1 - lam * a2).T
        o = o * lax.rsqrt(jnp.mean(o * o, axis=-1, keepdims=True) + EPS) * subln_ref[...]
        o_ref[:, h * HEAD_W:(h + 1) * HEAD_W] = (o * (1.0 - lam_init)).astype(bf16)


def _diffattn(proj, lamv, subln, consts, batch, seq):
    nq = seq // TQ
    return pl.pallas_call(
        _diffattn_body,
        out_shape=jax.ShapeDtypeStruct((batch * seq, SPLIT_W), bf16),
        grid=(batch, nq),
        in_specs=[pl.BlockSpec((TQ, SPLIT_W), lambda b, i: (b * nq + i, 4)),
                  pl.BlockSpec((seq, SPLIT_W), lambda b, i: (b, 5)),
                  pl.BlockSpec((seq, SPLIT_W), lambda b, i: (b, 6)),
                  pl.BlockSpec((8, LANES), lambda b, i: (0, 0)),
                  pl.BlockSpec((1, HEAD_W), lambda b, i: (0, 0)),
                  pl.BlockSpec((1, LANES), lambda b, i: (0, 0))],
        out_specs=pl.BlockSpec((TQ, SPLIT_W), lambda b, i: (b * nq + i, 0)),
        scratch_shapes=[pltpu.VMEM((DIFF_HEADS, nq, V_ROWS, TQ), bf16),
                        pltpu.VMEM((DIFF_HEADS, TQ, TQ), f32),
                        pltpu.VMEM((2, TQ, HEAD_W), bf16),
                        pltpu.VMEM((DIFF_HEADS, K_AUG, 2 * TQ), bf16),
                        pltpu.VMEM((2, DIFF_HEADS, TQ, 2 * TQ), f32),
                        pltpu.VMEM((N_MAPS, V_ROWS, TQ), f32),
                        pltpu.VMEM((2, N_MAPS, 1, TQ), f32)],
        compiler_params=_params("parallel", "arbitrary"),
        name="diffattn",
    )(proj, proj, proj, lamv, subln, consts)


def _route_rows(lt):
    lg = [lt[r:r + 1, :] for r in range(N_GROUPS)]
    best, gidx = lg[0], jnp.zeros(lg[0].shape, i32)
    for r in range(1, N_GROUPS):
        upd = lg[r] > best
        gidx = jnp.where(upd, r, gidx)
        best = jnp.where(upd, lg[r], best)
    den = sum(jnp.exp(v - best) for v in lg)
    g_w = 1.0 / den

    def sel(e):
        rows = [lt[8 + 4 * g + e:9 + 4 * g + e, :] for g in range(N_GROUPS)]
        return jnp.where(gidx == 0, rows[0], jnp.where(gidx == 1, rows[1], jnp.where(gidx == 2, rows[2], rows[3])))

    le = [sel(e) for e in range(EXPERTS_PER_GROUP)]
    v1, i1 = le[0], jnp.zeros(le[0].shape, i32)
    for e in range(1, EXPERTS_PER_GROUP):
        upd = le[e] > v1
        i1 = jnp.where(upd, e, i1)
        v1 = jnp.where(upd, le[e], v1)
    rest = [jnp.where(i1 == e, -jnp.inf, le[e]) for e in range(EXPERTS_PER_GROUP)]
    v2, i2 = rest[0], jnp.zeros(le[0].shape, i32)
    for e in range(1, EXPERTS_PER_GROUP):
        upd = rest[e] > v2
        i2 = jnp.where(upd, e, i2)
        v2 = jnp.where(upd, rest[e], v2)
    e2 = jnp.exp(v2 - v1)
    w1 = (1.0 / (1.0 + e2)) * g_w
    w2 = (e2 / (1.0 + e2)) * g_w
    first_low = i1 < i2
    lo = jnp.minimum(i1, i2)
    hi = jnp.maximum(i1, i2)
    pair = jnp.where(lo == 0, hi - 1, jnp.where(lo == 1, hi + 1, 5))
    return gidx * N_PAIRS + pair, jnp.where(first_low, w1, w2), jnp.where(first_low, w2, w1)


def _outproj_body(ret_ref, diff_ref, wo_ref, x_ref, g_ref, wr_ref, rb_ref,
                  xw_ref, route_ref, cnt_ref, run_ref, tri_ref):
    @pl.when(pl.program_id(0) == 0)
    def _():
        run_ref[...] = jnp.zeros_like(run_ref)
        earlier = lax.broadcasted_iota(i32, (TM, TM), 0) < lax.broadcasted_iota(i32, (TM, TM), 1)
        tri_ref[...] = jnp.where(earlier, 1.0, 0.0).astype(bf16)

    x1 = x_ref[...] + _dot(ret_ref[...], wo_ref[:SPLIT_W, :]) + _dot(diff_ref[...], wo_ref[SPLIT_W:, :])
    for c in range(X_CHUNKS):
        xw_ref[c] = x1[:, c * LANES:(c + 1) * LANES]

    h = _rms(x1, g_ref[...])
    h_hi = h.astype(bf16)
    h_lo = (h - h_hi.astype(f32)).astype(bf16)
    r1 = _dot(h_hi, wr_ref[...])
    logits = r1[:, :LANES] + (r1[:, LANES:] + _dot(h_lo, wr_ref[:, :LANES])) + rb_ref[...]
    bucket, w_low, w_high = _route_rows(logits.T)

    sub = lax.broadcasted_iota(i32, (LANES, TM), 0)
    wt = jnp.where(sub == 0, w_low, jnp.where(sub == 1, w_high, 0.0))
    xw_ref[X_CHUNKS] = wt.T

    onehot = lax.broadcasted_iota(i32, (BUCKET_ROWS, TM), 0) == bucket
    oh_b = jnp.where(onehot, 1.0, 0.0).astype(bf16)
    before = _dot(oh_b, tri_ref[...])
    run = run_ref[...]
    oh_f = jnp.where(onehot, 1.0, 0.0)
    rank = jnp.sum(oh_f * (before + run[:, 0:1]), axis=0, keepdims=True).astype(i32)
    sub8 = lax.broadcasted_iota(i32, (8, TM), 0)
    route_ref[0] = jnp.where(sub8 == 0, bucket, jnp.where(sub8 == 1, rank, 0))
    run = run + jnp.sum(oh_f, axis=1, keepdims=True)
    run_ref[...] = run
    cnt_ref[...] = run


def _outproj(ret, diff, wo, x, g, wr, rb):
    t = x.shape[0]
    nt = t // TM
    return pl.pallas_call(
        _outproj_body,
        out_shape=(jax.ShapeDtypeStruct((ROW_CHUNKS, t, LANES), f32),
                   jax.ShapeDtypeStruct((nt, 8, TM), i32),
                   jax.ShapeDtypeStruct((BUCKET_ROWS, LANES), f32)),
        grid=(nt,),
        in_specs=[pl.BlockSpec((TM, SPLIT_W), lambda i: (i, 0)),
                  pl.BlockSpec((TM, SPLIT_W), lambda i: (i, 0)),
                  pl.BlockSpec((D_MODEL, D_MODEL), lambda i: (0, 0)),
                  pl.BlockSpec((TM, D_MODEL), lambda i: (i, 0)),
                  pl.BlockSpec((1, D_MODEL), lambda i: (0, 0)),
                  pl.BlockSpec((D_MODEL, 2 * LANES), lambda i: (0, 0)),
                  pl.BlockSpec((1, LANES), lambda i: (0, 0))],
        out_specs=(pl.BlockSpec((ROW_CHUNKS, TM, LANES), lambda i: (0, i, 0)),
                   pl.BlockSpec((1, 8, TM), lambda i: (i, 0, 0)),
                   pl.BlockSpec((BUCKET_ROWS, LANES), lambda i: (0, 0))),
        scratch_shapes=[pltpu.VMEM((BUCKET_ROWS, LANES), f32),
                        pltpu.VMEM((TM, TM), bf16)],
        compiler_params=_params("arbitrary"),
        name="outproj_router",
    )(ret, diff, wo, x, g, wr, rb)


def _sc_mesh():
    return plsc.VectorSubcoreMesh(core_axis_name="core", subcore_axis_name="subcore")


def _chunk_rows(pos, chunks, rows_per_chunk):
    return (pos[None, :] + jnp.arange(chunks, dtype=i32)[:, None] * rows_per_chunk).reshape(1, -1)


def _sc_dispatch(pos, xw, rows_out):
    chunks, t, _ = xw.shape
    n = chunks * t

    @functools.partial(pl.kernel, out_type=jax.ShapeDtypeStruct((chunks * rows_out, LANES), xw.dtype),
                       mesh=_sc_mesh(), scratch_types=[])
    def run(x_hbm, i_hbm, o_hbm):
        def body(x_vmem, i_vmem):
            pltpu.sync_copy(x_vmem, o_hbm.at[i_vmem.at[0]])

        pltpu.emit_pipeline(
            body,
            grid=(n // SC_WINDOW,),
            in_specs=[pl.BlockSpec((SC_WINDOW, LANES), lambda i: (i, 0)),
                      pl.BlockSpec((1, SC_WINDOW), lambda i: (0, i))],
            out_specs=[],
            core_axis_name=("core", "subcore"),
            dimension_semantics=(pltpu.PARALLEL,),
        )(x_hbm, i_hbm)

    out = run(xw.reshape(n, LANES), _chunk_rows(pos, chunks, rows_out))
    return out.reshape(chunks, rows_out, LANES)


def _sc_combine(pos, ys, t):
    chunks, rows_in, _ = ys.shape
    n = chunks * t

    @functools.partial(pl.kernel, out_type=jax.ShapeDtypeStruct((n, LANES), ys.dtype),
                       mesh=_sc_mesh(), scratch_types=[])
    def run(y_hbm, i_hbm, o_hbm):
        def body(i_vmem, o_vmem):
            pltpu.sync_copy(y_hbm.at[i_vmem.at[0]], o_vmem)

        pltpu.emit_pipeline(
            body,
            grid=(n // SC_WINDOW,),
            in_specs=[pl.BlockSpec((1, SC_WINDOW), lambda i: (0, i))],
            out_specs=[pl.BlockSpec((SC_WINDOW, LANES), lambda i: (i, 0))],
            core_axis_name=("core", "subcore"),
            dimension_semantics=(pltpu.PARALLEL,),
        )(i_hbm, o_hbm)

    out = run(ys.reshape(chunks * rows_in, LANES), _chunk_rows(pos, chunks, rows_in))
    return out.reshape(chunks, t, LANES)


def _moe_body(e_lo_ref, e_hi_ref, valid_ref, xs_ref, g_ref, wg1, wu1, wd1, wg2, wu2, wd2, o_ref):
    nv = valid_ref[pl.program_id(0)]

    @pl.when(nv > 0)
    def _():
        ok = lax.broadcasted_iota(i32, (TM_MOE, 1), 0) < nv
        x = jnp.where(ok, jnp.concatenate([xs_ref[c] for c in range(X_CHUNKS)], axis=1), 0.0)
        w_lo = jnp.where(ok, xs_ref[X_CHUNKS, :, 0:1], 0.0)
        w_hi = jnp.where(ok, xs_ref[X_CHUNKS, :, 1:2], 0.0)
        h = _rms(x, g_ref[...]).astype(bf16)
        cast = lambda w_ref: w_ref[...].astype(bf16)
        hid1 = jax.nn.silu(_dot(h, cast(wg1))) * _dot(h, cast(wu1)) * w_lo
        hid2 = jax.nn.silu(_dot(h, cast(wg2))) * _dot(h, cast(wu2)) * w_hi
        y = x + (_dot(hid1.astype(bf16), cast(wd1)) + _dot(hid2.astype(bf16), cast(wd2)))
        for c in range(X_CHUNKS):
            o_ref[c] = y[:, c * LANES:(c + 1) * LANES]

    @pl.when(nv == 0)
    def _():
        o_ref[...] = jnp.zeros_like(o_ref)


def _moe(layer, e_lo, e_hi, valid, xs, g, wg, wu, wd):
    n_tiles = xs.shape[1] // TM_MOE
    up = lambda sel: pl.BlockSpec((None, None, D_MODEL, D_EXPERT),
                                  lambda i, lo, hi, nv: (layer, (lo, hi)[sel][i], 0, 0))
    down = lambda sel: pl.BlockSpec((None, None, D_EXPERT, D_MODEL),
                                    lambda i, lo, hi, nv: (layer, (lo, hi)[sel][i], 0, 0))
    return pl.pallas_call(
        _moe_body,
        out_shape=jax.ShapeDtypeStruct((X_CHUNKS, xs.shape[1], LANES), f32),
        grid_spec=pltpu.PrefetchScalarGridSpec(
            num_scalar_prefetch=3,
            grid=(n_tiles,),
            in_specs=[pl.BlockSpec((ROW_CHUNKS, TM_MOE, LANES), lambda i, lo, hi, nv: (0, i, 0)),
                      pl.BlockSpec((1, D_MODEL), lambda i, lo, hi, nv: (0, 0)),
                      up(0), up(0), down(0), up(1), up(1), down(1)],
            out_specs=pl.BlockSpec((X_CHUNKS, TM_MOE, LANES), lambda i, lo, hi, nv: (0, i, 0))),
        compiler_params=_params("arbitrary"),
        name="moe",
    )(e_lo, e_hi, valid, xs, g, wg, wu, wd, wg, wu, wd)


def _sort_plan(route, counts_f, n_tiles):
    bucket = route[:, 0, :].reshape(-1)
    rank = route[:, 1, :].reshape(-1)
    counts = counts_f[:N_BUCKETS, 0].astype(i32)
    tiles_per = (counts + TM_MOE - 1) // TM_MOE
    tile_end = jnp.cumsum(tiles_per)
    tile_start = tile_end - tiles_per
    pos = (tile_start * TM_MOE)[bucket] + rank
    tile = jnp.arange(n_tiles, dtype=i32)
    used = tile < tile_end[-1]
    owner = lambda tl: jnp.minimum(jnp.sum((tl[:, None] >= tile_end[None, :]).astype(i32), axis=1), N_BUCKETS - 1)
    tb = owner(jnp.where(used, tile, tile_end[-1] - 1))
    valid = jnp.where(used, jnp.clip(counts[tb] - (tile - tile_start[tb]) * TM_MOE, 0, TM_MOE), 0).astype(i32)
    pair_lo = jnp.array([0, 0, 0, 1, 1, 2], i32)
    pair_hi = jnp.array([1, 2, 3, 2, 3, 3], i32)
    grp = tb // N_PAIRS
    e_lo = grp * EXPERTS_PER_GROUP + pair_lo[tb % N_PAIRS]
    e_hi = grp * EXPERTS_PER_GROUP + pair_hi[tb % N_PAIRS]
    return pos.astype(i32), e_lo, e_hi, valid


def _ple_body(x_ref, p_ref, g_ref, wg_ref, wp_ref, gf_ref, o_ref, *, final):
    x = jnp.concatenate([x_ref[c] for c in range(X_CHUNKS)], axis=1)
    gate = jax.nn.sigmoid(_dot(_rms(x, g_ref[...]).astype(bf16), wg_ref[...]))
    y = x + gate * _dot(p_ref[...].astype(bf16), wp_ref[...])
    o_ref[...] = _rms(y, gf_ref[...]) if final else y


def _ple(layer, x, p, g, wg, wp, gf, final):
    t = x.shape[1]
    nt = t // TM
    return pl.pallas_call(
        functools.partial(_ple_body, final=final),
        out_shape=jax.ShapeDtypeStruct((t, D_MODEL), f32),
        grid=(nt,),
        in_specs=[pl.BlockSpec((X_CHUNKS, TM, LANES), lambda i: (0, i, 0)),
                  pl.BlockSpec((TM, D_PLE), lambda i: (layer * nt + i, 0)),
                  pl.BlockSpec((1, D_MODEL), lambda i: (0, 0)),
                  pl.BlockSpec((D_MODEL, D_MODEL), lambda i: (0, 0)),
                  pl.BlockSpec((D_PLE, D_MODEL), lambda i: (0, 0)),
                  pl.BlockSpec((1, D_MODEL), lambda i: (0, 0))],
        out_specs=pl.BlockSpec((TM, D_MODEL), lambda i: (i, 0)),
        compiler_params=_params("parallel"),
        name="ple_final" if final else "ple",
    )(x, p, g, wg, wp, gf)


def _router_weights(w_rg, b_rg, w_re, b_re):
    w = jnp.zeros((D_MODEL, LANES), f32)
    w = w.at[:, :N_GROUPS].set(w_rg).at[:, 8:8 + N_EXPERTS].set(w_re)
    hi = w.astype(bf16)
    lo = (w - hi.astype(f32)).astype(bf16)
    b = jnp.zeros((1, LANES), f32).at[0, :N_GROUPS].set(b_rg).at[0, 8:8 + N_EXPERTS].set(b_re)
    return jnp.concatenate([hi, lo], axis=1), b


def kernel(x, p, norm_mix, w_in, lam_q1, lam_k1, lam_q2, lam_k2, diff_subln, w_out, norm_ffn,
           w_router_group, b_router_group, w_router_expert, b_router_expert, w_gate, w_up, w_down,
           norm_ple, w_ple_gate, w_ple_proj, norm_final):
    batch, seq, _ = x.shape
    t = batch * seq
    n_tiles = t // TM_MOE + N_BUCKETS
    xf = x.reshape(t, D_MODEL)
    p_rows = p.reshape(DEPTH * t, D_PLE)
    row = lambda v: v.reshape(1, -1).astype(f32)
    for i in range(DEPTH):
        lam_init = 0.8 - 0.6 * math.exp(-0.3 * i)
        proj = _inproj(xf, row(norm_mix[i]), w_in[i].astype(bf16))
        ret = _retention(proj, batch, seq)
        lamv = jnp.zeros((8, LANES), f32)
        for r, v in enumerate((lam_q1[i], lam_k1[i], lam_q2[i], lam_k2[i])):
            lamv = lamv.at[r, :DIFF_DH].set(v.astype(f32))
        consts = jnp.full((1, LANES), lam_init, f32)
        diff = _diffattn(proj, lamv, row(diff_subln[i]), consts, batch, seq)
        wr, rb = _router_weights(w_router_group[i], b_router_group[i], w_router_expert[i], b_router_expert[i])
        xw, route, counts = _outproj(ret, diff, w_out[i].astype(bf16), xf, row(norm_ffn[i]), wr, rb)
        pos, e_lo, e_hi, valid = _sort_plan(route, counts, n_tiles)
        xs = _sc_dispatch(pos, xw, n_tiles * TM_MOE)
        ys = _moe(i, e_lo, e_hi, valid, xs, row(norm_ffn[i]), w_gate, w_up, w_down)
        x2 = _sc_combine(pos, ys, t)
        xf = _ple(i, x2, p_rows, row(norm_ple[i]), w_ple_gate[i].astype(bf16),
                  w_ple_proj[i].astype(bf16), row(norm_final), final=(i == DEPTH - 1))
    return xf.reshape(batch, seq, D_MODEL)
```

```python
import functools
import math

import jax
import jax.numpy as jnp
from jax import lax
from jax.experimental import pallas as pl
from jax.experimental.pallas import tpu as pltpu
from jax.experimental.pallas import tpu_sc as plsc

f32 = jnp.float32
bf16 = jnp.bfloat16
i32 = jnp.int32

D_MODEL = 1024
DEPTH = 4
CHUNK = 64
D_PLE = 256
EPS = 1e-6
RET_HEADS = 4
RET_DK = 128
DIFF_HEADS = 4
DIFF_DH = 64
HEAD_W = 128
SPLIT_W = 512
IN_COLS = 7 * SPLIT_W
N_GROUPS = 4
EXPERTS_PER_GROUP = 4
N_EXPERTS = 16
D_EXPERT = 512
N_PAIRS = 6
N_BUCKETS = N_GROUPS * N_PAIRS
BUCKET_ROWS = 32

LANES = 128
X_CHUNKS = D_MODEL // LANES
ROW_CHUNKS = X_CHUNKS + 1

TM = 512
RET_BLOCK = 256
TQ = 256
STRIP = 64
TM_MOE = 512
SC_WINDOW = 128
VMEM_LIMIT = 56 * 1024 * 1024
NEG_BIG = -1e30


def _rms(x, g):
    return x * lax.rsqrt(jnp.mean(x * x, axis=-1, keepdims=True) + EPS) * g


def _dot(a, b):
    return jnp.dot(a, b, preferred_element_type=f32)


def _dot_nt(a, b):
    return lax.dot_general(a, b, (((1,), (1,)), ((), ())), preferred_element_type=f32)


def _dot_tn(a, b):
    return lax.dot_general(a, b, (((0,), (0,)), ((), ())), preferred_element_type=f32)


def _params(*sem):
    return pltpu.CompilerParams(dimension_semantics=sem, vmem_limit_bytes=VMEM_LIMIT)


def _inproj_body(x_ref, g_ref, w_ref, o_ref):
    h = _rms(x_ref[...], g_ref[...]).astype(bf16)
    for n in range(IN_COLS // SPLIT_W):
        cols = slice(n * SPLIT_W, (n + 1) * SPLIT_W)
        o_ref[:, cols] = _dot(h, w_ref[:, cols]).astype(bf16)


def _inproj(x, g, w):
    t = x.shape[0]
    return pl.pallas_call(
        _inproj_body,
        out_shape=jax.ShapeDtypeStruct((t, IN_COLS), bf16),
        grid=(t // TM,),
        in_specs=[pl.BlockSpec((TM, D_MODEL), lambda i: (i, 0)),
                  pl.BlockSpec((1, D_MODEL), lambda i: (0, 0)),
                  pl.BlockSpec((D_MODEL, IN_COLS), lambda i: (0, 0))],
        out_specs=pl.BlockSpec((TM, IN_COLS), lambda i: (i, 0)),
        compiler_params=_params("parallel"),
        name="inproj",
    )(x, g, w)


def _retention_body(q_ref, k_ref, v_ref, g_ref, o_ref, state_ref, decay_ref, qdec_ref, kdec_ref):
    n = RET_BLOCK
    scale = RET_DK ** -0.5
    log_gamma = [math.log1p(-(2.0 ** (-5.0 - h))) for h in range(RET_HEADS)]

    @pl.when(pl.program_id(1) == 0)
    def _():
        state_ref[...] = jnp.zeros_like(state_ref)
        rel = (lax.broadcasted_iota(i32, (n, n), 0) - lax.broadcasted_iota(i32, (n, n), 1)).astype(f32)
        pos = lax.broadcasted_iota(i32, (n, HEAD_W), 0).astype(f32)
        for h, log_g in enumerate(log_gamma):
            decay_ref[h] = jnp.where(rel >= 0, jnp.exp(log_g * jnp.maximum(rel, 0.0)), 0.0) * scale
            qdec_ref[h] = jnp.exp(log_g * (pos + 1.0))
            kdec_ref[h] = jnp.exp(log_g * (n - 1.0 - pos)) * scale

    for h, log_g in enumerate(log_gamma):
        cols = slice(h * HEAD_W, (h + 1) * HEAD_W)
        q = q_ref[:, cols]
        k = k_ref[:, cols]
        v = v_ref[:, cols]
        scores = _dot_nt(q, k) * decay_ref[h]
        intra = _dot(scores.astype(bf16), v)
        state = state_ref[h]
        cross = _dot(q, state.astype(bf16)) * qdec_ref[h]
        k_dec = (k.astype(f32) * kdec_ref[h]).astype(bf16)
        state_ref[h] = math.exp(log_g * n) * state + _dot_tn(k_dec, v)
        o = intra + cross
        o = o * lax.rsqrt(jnp.mean(o * o, axis=-1, keepdims=True) + EPS)
        gate = g_ref[:, cols].astype(f32)
        o_ref[:, cols] = (gate * jax.nn.sigmoid(gate) * o).astype(bf16)


def _retention(proj, batch, seq):
    nb = seq // RET_BLOCK
    spec = lambda c: pl.BlockSpec((RET_BLOCK, SPLIT_W), lambda b, j: (b * nb + j, c))
    return pl.pallas_call(
        _retention_body,
        out_shape=jax.ShapeDtypeStruct((batch * seq, SPLIT_W), bf16),
        grid=(batch, nb),
        in_specs=[spec(0), spec(1), spec(2), spec(3)],
        out_specs=pl.BlockSpec((RET_BLOCK, SPLIT_W), lambda b, j: (b * nb + j, 0)),
        scratch_shapes=[pltpu.VMEM((RET_HEADS, RET_DK, HEAD_W), f32),
                        pltpu.VMEM((RET_HEADS, RET_BLOCK, RET_BLOCK), f32),
                        pltpu.VMEM((RET_HEADS, RET_BLOCK, HEAD_W), f32),
                        pltpu.VMEM((RET_HEADS, RET_BLOCK, HEAD_W), f32)],
        compiler_params=_params("parallel", "arbitrary"),
        name="retention",
    )(proj, proj, proj, proj)


_STRIPS = [slice(s * STRIP, (s + 1) * STRIP) for s in range(TQ // STRIP)]
_fold8 = lambda a: a.reshape(STRIP // 8, 8, TQ)


N_MAPS = 2 * DIFF_HEADS
_SLOPES = [2.0 ** (-8.0 * (h + 1) / DIFF_HEADS) for h in range(DIFF_HEADS)]
_M_OLD, _M_NEW = 0, 1
V_ROWS = HEAD_W + 16
K_AUG = 2 * HEAD_W


_ALL_HEADS = tuple(range(DIFF_HEADS))


def _raw_scores(k_ref, feat_ref, blk, qa_ref, raw_ref, heads=_ALL_HEADS):
    feat = feat_ref[...]
    for h in heads:
        kb = k_ref[pl.ds(pl.multiple_of(blk * TQ, TQ), TQ), h * HEAD_W:(h + 1) * HEAD_W]
        raw_ref[h] = _dot(jnp.concatenate([kb, feat], axis=1), qa_ref[h])


def _advance_max(raw_ref, bias_ref, dist, m_ref, heads=_ALL_HEADS):
    for h in heads:
        for c in range(2):
            m8 = None
            for rows in _STRIPS:
                t = raw_ref[h, rows, c * TQ:(c + 1) * TQ]
                if bias_ref is not None:
                    t = t + bias_ref[h, rows, :]
                g = jnp.max(_fold8(t), axis=0)
                m8 = g if m8 is None else jnp.maximum(m8, g)
            prev = m_ref[_M_NEW, 2 * h + c]
            m_ref[_M_OLD, 2 * h + c] = prev
            m_ref[_M_NEW, 2 * h + c] = jnp.maximum(prev, jnp.max(m8, axis=0, keepdims=True) - _SLOPES[h] * dist)


def _pipeline_step(k_ref, qa_ref, vt_ref, acc_ref, m_ref, cur, nxt):
    raw_c, bias_c, dist_c, blk_c = cur
    raw_n, bias_n, dist_n, blk_n, feat_n = nxt
    for h in _ALL_HEADS:
        _accumulate(raw_c, bias_c, dist_c, m_ref, vt_ref, blk_c, acc_ref, heads=(h,))
        _raw_scores(k_ref, feat_n, blk_n, qa_ref, raw_n, heads=(h,))
        if h > 0:
            _advance_max(raw_n, bias_n, dist_n, m_ref, heads=(h - 1,))
    _advance_max(raw_n, bias_n, dist_n, m_ref, heads=(DIFF_HEADS - 1,))


def _accumulate(raw_ref, bias_ref, dist, m_ref, vt_ref, blk, acc_ref, heads=_ALL_HEADS):
    for h in heads:
        for c in range(2):
            m = 2 * h + c
            m_old, m_new = m_ref[_M_OLD, m], m_ref[_M_NEW, m]
            shift = -_SLOPES[h] * dist - m_new
            p = []
            for rows in _STRIPS:
                t = raw_ref[h, rows, c * TQ:(c + 1) * TQ]
                if bias_ref is not None:
                    t = t + bias_ref[h, rows, :]
                p.append(jnp.exp(t + shift).astype(bf16))
            pv = _dot(vt_ref[h, blk], jnp.concatenate(p, axis=0))
            acc_ref[m] = jnp.exp(m_old - m_new) * acc_ref[m] + pv


def _diffattn_body(q_ref, k_ref, v_ref, lamv_ref, subln_ref, const_ref, o_ref,
                   vt_ref, bias_ref, feat_ref, qa_ref, raw_ref, acc_ref, m_ref):
    qi = pl.program_id(1)

    @pl.when(qi == 0)
    def _():
        key = lax.broadcasted_iota(i32, (TQ, TQ), 0)
        qry = lax.broadcasted_iota(i32, (TQ, TQ), 1)
        dist_in = jnp.abs(qry - key).astype(f32)
        allowed = (key // CHUNK) <= (qry // CHUNK)
        ones_rows = jnp.where(lax.broadcasted_iota(i32, (V_ROWS - HEAD_W, TQ), 0) == 0, 1.0, 0.0).astype(bf16)
        sub = lax.broadcasted_iota(i32, (HEAD_W, 2 * TQ), 0)
        qpos = (lax.broadcasted_iota(i32, (HEAD_W, 2 * TQ), 1) % TQ).astype(f32)
        for h in range(DIFF_HEADS):
            cols = slice(h * HEAD_W, (h + 1) * HEAD_W)
            for j in range(vt_ref.shape[1]):
                vt_ref[h, j, :HEAD_W, :] = v_ref[j * TQ:(j + 1) * TQ, cols].astype(f32).T.astype(bf16)
                vt_ref[h, j, HEAD_W:, :] = ones_rows
            bias_ref[h] = jnp.where(allowed, -_SLOPES[h] * dist_in, NEG_BIG)
            qa_ref[h, HEAD_W:, :] = jnp.where(sub == 0, _SLOPES[h],
                                              jnp.where(sub == 1, -_SLOPES[h] * qpos, 0.0)).astype(bf16)
        lane = lax.broadcasted_iota(i32, (TQ, HEAD_W), 1)
        krow = lax.broadcasted_iota(i32, (TQ, HEAD_W), 0).astype(f32)
        feat_ref[0] = jnp.where(lane == 0, krow, jnp.where(lane == 1, 1.0, 0.0)).astype(bf16)
        feat_ref[1] = jnp.zeros((TQ, HEAD_W), bf16)

    half = lax.broadcasted_iota(i32, (HEAD_W, TQ), 0) < DIFF_DH
    for h in range(DIFF_HEADS):
        q_t = (q_ref[:, h * HEAD_W:(h + 1) * HEAD_W].astype(f32) * (DIFF_DH ** -0.5)).T
        qa_ref[h, :HEAD_W, :] = jnp.concatenate([jnp.where(half, q_t, 0.0), jnp.where(half, 0.0, q_t)],
                                                axis=1).astype(bf16)
    acc_ref[...] = jnp.zeros_like(acc_ref)
    m_ref[_M_NEW] = jnp.full(m_ref.shape[1:], NEG_BIG, f32)

    dist = lambda j: ((qi - j) * TQ).astype(f32)
    zero = jnp.zeros((), f32)
    before, diagonal = feat_ref.at[0], feat_ref.at[1]

    @pl.when(qi == 0)
    def _():
        _raw_scores(k_ref, diagonal, 0, qa_ref, raw_ref.at[0])
        _advance_max(raw_ref.at[0], bias_ref, zero, m_ref)

    @pl.when(qi > 0)
    def _():
        _raw_scores(k_ref, before, 0, qa_ref, raw_ref.at[0])
        _advance_max(raw_ref.at[0], None, dist(0), m_ref)

    def step(j, carry):
        _pipeline_step(k_ref, qa_ref, vt_ref, acc_ref, m_ref,
                       cur=(raw_ref.at[j % 2], None, dist(j), j),
                       nxt=(raw_ref.at[(j + 1) % 2], None, dist(j + 1), j + 1, before))
        return carry

    lax.fori_loop(0, qi - 1, step, 0)

    @pl.when(qi > 0)
    def _():
        _pipeline_step(k_ref, qa_ref, vt_ref, acc_ref, m_ref,
                       cur=(raw_ref.at[(qi - 1) % 2], None, dist(qi - 1), qi - 1),
                       nxt=(raw_ref.at[qi % 2], bias_ref, zero, qi, diagonal))

    _accumulate(raw_ref.at[qi % 2], bias_ref, zero, m_ref, vt_ref, qi, acc_ref)

    lamv = lamv_ref[...]
    lam_init = const_ref[0:1, 0:1]
    lam = (jnp.exp(jnp.sum(lamv[0:1] * lamv[1:2], axis=-1, keepdims=True))
           - jnp.exp(jnp.sum(lamv[2:3] * lamv[3:4], axis=-1, keepdims=True)) + lam_init)
    for h in range(DIFF_HEADS):
        a1, a2 = (acc_ref[2 * h + c, :HEAD_W, :] / acc_ref[2 * h + c, HEAD_W:HEAD_W + 1, :] for c in range(2))
        o = (a1 - lam * a2).T
        o = o * lax.rsqrt(jnp.mean(o * o, axis=-1, keepdims=True) + EPS) * subln_ref[...]
        o_ref[:, h * HEAD_W:(h + 1) * HEAD_W] = (o * (1.0 - lam_init)).astype(bf16)


def _diffattn(proj, lamv, subln, consts, batch, seq):
    nq = seq // TQ
    return pl.pallas_call(
        _diffattn_body,
        out_shape=jax.ShapeDtypeStruct((batch * seq, SPLIT_W), bf16),
        grid=(batch, nq),
        in_specs=[pl.BlockSpec((TQ, SPLIT_W), lambda b, i: (b * nq + i, 4)),
                  pl.BlockSpec((seq, SPLIT_W), lambda b, i: (b, 5)),
                  pl.BlockSpec((seq, SPLIT_W), lambda b, i: (b, 6)),
                  pl.BlockSpec((8, LANES), lambda b, i: (0, 0)),
                  pl.BlockSpec((1, HEAD_W), lambda b, i: (0, 0)),
                  pl.BlockSpec((1, LANES), lambda b, i: (0, 0))],
        out_specs=pl.BlockSpec((TQ, SPLIT_W), lambda b, i: (b * nq + i, 0)),
        scratch_shapes=[pltpu.VMEM((DIFF_HEADS, nq, V_ROWS, TQ), bf16),
                        pltpu.VMEM((DIFF_HEADS, TQ, TQ), f32),
                        pltpu.VMEM((2, TQ, HEAD_W), bf16),
                        pltpu.VMEM((DIFF_HEADS, K_AUG, 2 * TQ), bf16),
                        pltpu.VMEM((2, DIFF_HEADS, TQ, 2 * TQ), f32),
                        pltpu.VMEM((N_MAPS, V_ROWS, TQ), f32),
                        pltpu.VMEM((2, N_MAPS, 1, TQ), f32)],
        compiler_params=_params("parallel", "arbitrary"),
        name="diffattn",
    )(proj, proj, proj, lamv, subln, consts)


def _route_rows(lt):
    lg = [lt[r:r + 1, :] for r in range(N_GROUPS)]
    best, gidx = lg[0], jnp.zeros(lg[0].shape, i32)
    for r in range(1, N_GROUPS):
        upd = lg[r] > best
        gidx = jnp.where(upd, r, gidx)
        best = jnp.where(upd, lg[r], best)
    den = sum(jnp.exp(v - best) for v in lg)
    g_w = 1.0 / den

    def sel(e):
        rows = [lt[8 + 4 * g + e:9 + 4 * g + e, :] for g in range(N_GROUPS)]
        return jnp.where(gidx == 0, rows[0], jnp.where(gidx == 1, rows[1], jnp.where(gidx == 2, rows[2], rows[3])))

    le = [sel(e) for e in range(EXPERTS_PER_GROUP)]
    v1, i1 = le[0], jnp.zeros(le[0].shape, i32)
    for e in range(1, EXPERTS_PER_GROUP):
        upd = le[e] > v1
        i1 = jnp.where(upd, e, i1)
        v1 = jnp.where(upd, le[e], v1)
    rest = [jnp.where(i1 == e, -jnp.inf, le[e]) for e in range(EXPERTS_PER_GROUP)]
    v2, i2 = rest[0], jnp.zeros(le[0].shape, i32)
    for e in range(1, EXPERTS_PER_GROUP):
        upd = rest[e] > v2
        i2 = jnp.where(upd, e, i2)
        v2 = jnp.where(upd, rest[e], v2)
    e2 = jnp.exp(v2 - v1)
    w1 = (1.0 / (1.0 + e2)) * g_w
    w2 = (e2 / (1.0 + e2)) * g_w
    first_low = i1 < i2
    lo = jnp.minimum(i1, i2)
    hi = jnp.maximum(i1, i2)
    pair = jnp.where(lo == 0, hi - 1, jnp.where(lo == 1, hi + 1, 5))
    return gidx * N_PAIRS + pair, jnp.where(first_low, w1, w2), jnp.where(first_low, w2, w1)


def _outproj_body(ret_ref, diff_ref, wo_ref, x_ref, g_ref, wr_ref, rb_ref,
                  xw_ref, route_ref, cnt_ref, run_ref, tri_ref):
    @pl.when(pl.program_id(0) == 0)
    def _():
        run_ref[...] = jnp.zeros_like(run_ref)
        earlier = lax.broadcasted_iota(i32, (TM, TM), 0) < lax.broadcasted_iota(i32, (TM, TM), 1)
        tri_ref[...] = jnp.where(earlier, 1.0, 0.0).astype(bf16)

    x1 = x_ref[...] + _dot(ret_ref[...], wo_ref[:SPLIT_W, :]) + _dot(diff_ref[...], wo_ref[SPLIT_W:, :])
    for c in range(X_CHUNKS):
        xw_ref[c] = x1[:, c * LANES:(c + 1) * LANES]

    h = _rms(x1, g_ref[...])
    h_hi = h.astype(bf16)
    h_lo = (h - h_hi.astype(f32)).astype(bf16)
    r1 = _dot(h_hi, wr_ref[...])
    logits = r1[:, :LANES] + (r1[:, LANES:] + _dot(h_lo, wr_ref[:, :LANES])) + rb_ref[...]
    bucket, w_low, w_high = _route_rows(logits.T)

    sub = lax.broadcasted_iota(i32, (LANES, TM), 0)
    wt = jnp.where(sub == 0, w_low, jnp.where(sub == 1, w_high, 0.0))
    xw_ref[X_CHUNKS] = wt.T

    onehot = lax.broadcasted_iota(i32, (BUCKET_ROWS, TM), 0) == bucket
    oh_b = jnp.where(onehot, 1.0, 0.0).astype(bf16)
    before = _dot(oh_b, tri_ref[...])
    run = run_ref[...]
    oh_f = jnp.where(onehot, 1.0, 0.0)
    rank = jnp.sum(oh_f * (before + run[:, 0:1]), axis=0, keepdims=True).astype(i32)
    sub8 = lax.broadcasted_iota(i32, (8, TM), 0)
    route_ref[0] = jnp.where(sub8 == 0, bucket, jnp.where(sub8 == 1, rank, 0))
    run = run + jnp.sum(oh_f, axis=1, keepdims=True)
    run_ref[...] = run
    cnt_ref[...] = run


def _outproj(ret, diff, wo, x, g, wr, rb):
    t = x.shape[0]
    nt = t // TM
    return pl.pallas_call(
        _outproj_body,
        out_shape=(jax.ShapeDtypeStruct((ROW_CHUNKS, t, LANES), f32),
                   jax.ShapeDtypeStruct((nt, 8, TM), i32),
                   jax.ShapeDtypeStruct((BUCKET_ROWS, LANES), f32)),
        grid=(nt,),
        in_specs=[pl.BlockSpec((TM, SPLIT_W), lambda i: (i, 0)),
                  pl.BlockSpec((TM, SPLIT_W), lambda i: (i, 0)),
                  pl.BlockSpec((D_MODEL, D_MODEL), lambda i: (0, 0)),
                  pl.BlockSpec((TM, D_MODEL), lambda i: (i, 0)),
                  pl.BlockSpec((1, D_MODEL), lambda i: (0, 0)),
                  pl.BlockSpec((D_MODEL, 2 * LANES), lambda i: (0, 0)),
                  pl.BlockSpec((1, LANES), lambda i: (0, 0))],
        out_specs=(pl.BlockSpec((ROW_CHUNKS, TM, LANES), lambda i: (0, i, 0)),
                   pl.BlockSpec((1, 8, TM), lambda i: (i, 0, 0)),
                   pl.BlockSpec((BUCKET_ROWS, LANES), lambda i: (0, 0))),
        scratch_shapes=[pltpu.VMEM((BUCKET_ROWS, LANES), f32),
                        pltpu.VMEM((TM, TM), bf16)],
        compiler_params=_params("arbitrary"),
        name="outproj_router",
    )(ret, diff, wo, x, g, wr, rb)


def _sc_mesh():
    return plsc.VectorSubcoreMesh(core_axis_name="core", subcore_axis_name="subcore")


def _chunk_rows(pos, chunks, rows_per_chunk):
    return (pos[None, :] + jnp.arange(chunks, dtype=i32)[:, None] * rows_per_chunk).reshape(1, -1)


def _sc_dispatch(pos, xw, rows_out):
    chunks, t, _ = xw.shape
    n = chunks * t

    @functools.partial(pl.kernel, out_type=jax.ShapeDtypeStruct((chunks * rows_out, LANES), xw.dtype),
                       mesh=_sc_mesh(), scratch_types=[])
    def run(x_hbm, i_hbm, o_hbm):
        def body(x_vmem, i_vmem):
            pltpu.sync_copy(x_vmem, o_hbm.at[i_vmem.at[0]])

        pltpu.emit_pipeline(
            body,
            grid=(n // SC_WINDOW,),
            in_specs=[pl.BlockSpec((SC_WINDOW, LANES), lambda i: (i, 0)),
                      pl.BlockSpec((1, SC_WINDOW), lambda i: (0, i))],
            out_specs=[],
            core_axis_name=("core", "subcore"),
            dimension_semantics=(pltpu.PARALLEL,),
        )(x_hbm, i_hbm)

    out = run(xw.reshape(n, LANES), _chunk_rows(pos, chunks, rows_out))
    return out.reshape(chunks, rows_out, LANES)


def _sc_combine(pos, ys, t):
    chunks, rows_in, _ = ys.shape
    n = chunks * t

    @functools.partial(pl.kernel, out_type=jax.ShapeDtypeStruct((n, LANES), ys.dtype),
                       mesh=_sc_mesh(), scratch_types=[])
    def run(y_hbm, i_hbm, o_hbm):
        def body(i_vmem, o_vmem):
            pltpu.sync_copy(y_hbm.at[i_vmem.at[0]], o_vmem)

        pltpu.emit_pipeline(
            body,
            grid=(n // SC_WINDOW,),
            in_specs=[pl.BlockSpec((1, SC_WINDOW), lambda i: (0, i))],
            out_specs=[pl.BlockSpec((SC_WINDOW, LANES), lambda i: (i, 0))],
            core_axis_name=("core", "subcore"),
            dimension_semantics=(pltpu.PARALLEL,),
        )(i_hbm, o_hbm)

    out = run(ys.reshape(chunks * rows_in, LANES), _chunk_rows(pos, chunks, rows_in))
    return out.reshape(chunks, t, LANES)


def _moe_body(e_lo_ref, e_hi_ref, valid_ref, xs_ref, g_ref, wg1, wu1, wd1, wg2, wu2, wd2, o_ref):
    nv = valid_ref[pl.program_id(0)]

    @pl.when(nv > 0)
    def _():
        ok = lax.broadcasted_iota(i32, (TM_MOE, 1), 0) < nv
        x = jnp.where(ok, jnp.concatenate([xs_ref[c] for c in range(X_CHUNKS)], axis=1), 0.0)
        w_lo = jnp.where(ok, xs_ref[X_CHUNKS, :, 0:1], 0.0)
        w_hi = jnp.where(ok, xs_ref[X_CHUNKS, :, 1:2], 0.0)
        h = _rms(x, g_ref[...]).astype(bf16)
        cast = lambda w_ref: w_ref[...].astype(bf16)
        hid1 = jax.nn.silu(_dot(h, cast(wg1))) * _dot(h, cast(wu1)) * w_lo
        hid2 = jax.nn.silu(_dot(h, cast(wg2))) * _dot(h, cast(wu2)) * w_hi
        y = x + (_dot(hid1.astype(bf16), cast(wd1)) + _dot(hid2.astype(bf16), cast(wd2)))
        for c in range(X_CHUNKS):
            o_ref[c] = y[:, c * LANES:(c + 1) * LANES]

    @pl.when(nv == 0)
    def _():
        o_ref[...] = jnp.zeros_like(o_ref)


def _moe(layer, e_lo, e_hi, valid, xs, g, wg, wu, wd):
    n_tiles = xs.shape[1] // TM_MOE
    up = lambda sel: pl.BlockSpec((None, None, D_MODEL, D_EXPERT),
                                  lambda i, lo, hi, nv: (layer, (lo, hi)[sel][i], 0, 0))
    down = lambda sel: pl.BlockSpec((None, None, D_EXPERT, D_MODEL),
                                    lambda i, lo, hi, nv: (layer, (lo, hi)[sel][i], 0, 0))
    return pl.pallas_call(
        _moe_body,
        out_shape=jax.ShapeDtypeStruct((X_CHUNKS, xs.shape[1], LANES), f32),
        grid_spec=pltpu.PrefetchScalarGridSpec(
            num_scalar_prefetch=3,
            grid=(n_tiles,),
            in_specs=[pl.BlockSpec((ROW_CHUNKS, TM_MOE, LANES), lambda i, lo, hi, nv: (0, i, 0)),
                      pl.BlockSpec((1, D_MODEL), lambda i, lo, hi, nv: (0, 0)),
                      up(0), up(0), down(0), up(1), up(1), down(1)],
            out_specs=pl.BlockSpec((X_CHUNKS, TM_MOE, LANES), lambda i, lo, hi, nv: (0, i, 0))),
        compiler_params=_params("arbitrary"),
        name="moe",
    )(e_lo, e_hi, valid, xs, g, wg, wu, wd, wg, wu, wd)


def _sort_plan(route, counts_f, n_tiles):
    bucket = route[:, 0, :].reshape(-1)
    rank = route[:, 1, :].reshape(-1)
    counts = counts_f[:N_BUCKETS, 0].astype(i32)
    tiles_per = (counts + TM_MOE - 1) // TM_MOE
    tile_end = jnp.cumsum(tiles_per)
    tile_start = tile_end - tiles_per
    pos = (tile_start * TM_MOE)[bucket] + rank
    tile = jnp.arange(n_tiles, dtype=i32)
    used = tile < tile_end[-1]
    owner = lambda tl: jnp.minimum(jnp.sum((tl[:, None] >= tile_end[None, :]).astype(i32), axis=1), N_BUCKETS - 1)
    tb = owner(jnp.where(used, tile, tile_end[-1] - 1))
    valid = jnp.where(used, jnp.clip(counts[tb] - (tile - tile_start[tb]) * TM_MOE, 0, TM_MOE), 0).astype(i32)
    pair_lo = jnp.array([0, 0, 0, 1, 1, 2], i32)
    pair_hi = jnp.array([1, 2, 3, 2, 3, 3], i32)
    grp = tb // N_PAIRS
    e_lo = grp * EXPERTS_PER_GROUP + pair_lo[tb % N_PAIRS]
    e_hi = grp * EXPERTS_PER_GROUP + pair_hi[tb % N_PAIRS]
    return pos.astype(i32), e_lo, e_hi, valid


def _ple_body(x_ref, p_ref, g_ref, wg_ref, wp_ref, gf_ref, o_ref, *, final):
    x = jnp.concatenate([x_ref[c] for c in range(X_CHUNKS)], axis=1)
    gate = jax.nn.sigmoid(_dot(_rms(x, g_ref[...]).astype(bf16), wg_ref[...]))
    y = x + gate * _dot(p_ref[...].astype(bf16), wp_ref[...])
    o_ref[...] = _rms(y, gf_ref[...]) if final else y


def _ple(layer, x, p, g, wg, wp, gf, final):
    t = x.shape[1]
    nt = t // TM
    return pl.pallas_call(
        functools.partial(_ple_body, final=final),
        out_shape=jax.ShapeDtypeStruct((t, D_MODEL), f32),
        grid=(nt,),
        in_specs=[pl.BlockSpec((X_CHUNKS, TM, LANES), lambda i: (0, i, 0)),
                  pl.BlockSpec((TM, D_PLE), lambda i: (layer * nt + i, 0)),
                  pl.BlockSpec((1, D_MODEL), lambda i: (0, 0)),
                  pl.BlockSpec((D_MODEL, D_MODEL), lambda i: (0, 0)),
                  pl.BlockSpec((D_PLE, D_MODEL), lambda i: (0, 0)),
                  pl.BlockSpec((1, D_MODEL), lambda i: (0, 0))],
        out_specs=pl.BlockSpec((TM, D_MODEL), lambda i: (i, 0)),
        compiler_params=_params("parallel"),
        name="ple_final" if final else "ple",
    )(x, p, g, wg, wp, gf)


def _router_weights(w_rg, b_rg, w_re, b_re):
    w = jnp.zeros((D_MODEL, LANES), f32)
    w = w.at[:, :N_GROUPS].set(w_rg).at[:, 8:8 + N_EXPERTS].set(w_re)
    hi = w.astype(bf16)
    lo = (w - hi.astype(f32)).astype(bf16)
    b = jnp.zeros((1, LANES), f32).at[0, :N_GROUPS].set(b_rg).at[0, 8:8 + N_EXPERTS].set(b_re)
    return jnp.concatenate([hi, lo], axis=1), b


def kernel(x, p, norm_mix, w_in, lam_q1, lam_k1, lam_q2, lam_k2, diff_subln, w_out, norm_ffn,
           w_router_group, b_router_group, w_router_expert, b_router_expert, w_gate, w_up, w_down,
           norm_ple, w_ple_gate, w_ple_proj, norm_final):
    batch, seq, _ = x.shape
    t = batch * seq
    n_tiles = t // TM_MOE + N_BUCKETS
    xf = x.reshape(t, D_MODEL)
    p_rows = p.reshape(DEPTH * t, D_PLE)
    row = lambda v: v.reshape(1, -1).astype(f32)
    for i in range(DEPTH):
        lam_init = 0.8 - 0.6 * math.exp(-0.3 * i)
        proj = _inproj(xf, row(norm_mix[i]), w_in[i].astype(bf16))
        ret = _retention(proj, batch, seq)
        lamv = jnp.zeros((8, LANES), f32)
        for r, v in enumerate((lam_q1[i], lam_k1[i], lam_q2[i], lam_k2[i])):
            lamv = lamv.at[r, :DIFF_DH].set(v.astype(f32))
        consts = jnp.full((1, LANES), lam_init, f32)
        diff = _diffattn(proj, lamv, row(diff_subln[i]), consts, batch, seq)
        wr, rb = _router_weights(w_router_group[i], b_router_group[i], w_router_expert[i], b_router_expert[i])
        xw, route, counts = _outproj(ret, diff, w_out[i].astype(bf16), xf, row(norm_ffn[i]), wr, rb)
        pos, e_lo, e_hi, valid = _sort_plan(route, counts, n_tiles)
        xs = _sc_dispatch(pos, xw, n_tiles * TM_MOE)
        ys = _moe(i, e_lo, e_hi, valid, xs, row(norm_ffn[i]), w_gate, w_up, w_down)
        x2 = _sc_combine(pos, ys, t)
        xf = _ple(i, x2, p_rows, row(norm_ple[i]), w_ple_gate[i].astype(bf16),
                  w_ple_proj[i].astype(bf16), row(norm_final), final=(i == DEPTH - 1))
    return xf.reshape(batch, seq, D_MODEL)
```

```python
import functools
import math

import jax
import jax.numpy as jnp
from jax import lax
from jax.experimental import pallas as pl
from jax.experimental.pallas import tpu as pltpu
from jax.experimental.pallas import tpu_sc as plsc

f32 = jnp.float32
bf16 = jnp.bfloat16
i32 = jnp.int32

D_MODEL = 1024
DEPTH = 4
CHUNK = 64
D_PLE = 256
EPS = 1e-6
RET_HEADS = 4
RET_DK = 128
DIFF_HEADS = 4
DIFF_DH = 64
HEAD_W = 128
SPLIT_W = 512
IN_COLS = 7 * SPLIT_W
N_GROUPS = 4
EXPERTS_PER_GROUP = 4
N_EXPERTS = 16
D_EXPERT = 512
N_PAIRS = 6
N_BUCKETS = N_GROUPS * N_PAIRS
BUCKET_ROWS = 32

LANES = 128
X_CHUNKS = D_MODEL // LANES
ROW_CHUNKS = X_CHUNKS + 1

TM = 512
RET_BLOCK = 256
TQ = 256
STRIP = 64
TM_MOE = 512
SC_WINDOW = 128
STREAMS = 2
VMEM_LIMIT = 56 * 1024 * 1024
NEG_BIG = -1e30


def _rms(x, g):
    return x * lax.rsqrt(jnp.mean(x * x, axis=-1, keepdims=True) + EPS) * g


def _dot(a, b):
    return jnp.dot(a, b, preferred_element_type=f32)


def _dot_nt(a, b):
    return lax.dot_general(a, b, (((1,), (1,)), ((), ())), preferred_element_type=f32)


def _dot_tn(a, b):
    return lax.dot_general(a, b, (((0,), (0,)), ((), ())), preferred_element_type=f32)


def _params(*sem):
    return pltpu.CompilerParams(dimension_semantics=sem, vmem_limit_bytes=VMEM_LIMIT)


def _inproj_body(x_ref, g_ref, w_ref, o_ref):
    h = _rms(x_ref[...], g_ref[...]).astype(bf16)
    for n in range(IN_COLS // SPLIT_W):
        cols = slice(n * SPLIT_W, (n + 1) * SPLIT_W)
        o_ref[:, cols] = _dot(h, w_ref[:, cols]).astype(bf16)


def _inproj(x, g, w, t, first_block):
    return pl.pallas_call(
        _inproj_body,
        out_shape=jax.ShapeDtypeStruct((t, IN_COLS), bf16),
        grid=(t // TM,),
        in_specs=[pl.BlockSpec((TM, D_MODEL), lambda i: (i + first_block, 0)),
                  pl.BlockSpec((1, D_MODEL), lambda i: (0, 0)),
                  pl.BlockSpec((D_MODEL, IN_COLS), lambda i: (0, 0))],
        out_specs=pl.BlockSpec((TM, IN_COLS), lambda i: (i, 0)),
        compiler_params=_params("parallel"),
        name="inproj",
    )(x, g, w)


def _retention_body(q_ref, k_ref, v_ref, g_ref, o_ref, state_ref, decay_ref, qdec_ref, kdec_ref):
    n = RET_BLOCK
    scale = RET_DK ** -0.5
    log_gamma = [math.log1p(-(2.0 ** (-5.0 - h))) for h in range(RET_HEADS)]

    @pl.when(pl.program_id(1) == 0)
    def _():
        state_ref[...] = jnp.zeros_like(state_ref)
        rel = (lax.broadcasted_iota(i32, (n, n), 0) - lax.broadcasted_iota(i32, (n, n), 1)).astype(f32)
        pos = lax.broadcasted_iota(i32, (n, HEAD_W), 0).astype(f32)
        for h, log_g in enumerate(log_gamma):
            decay_ref[h] = jnp.where(rel >= 0, jnp.exp(log_g * jnp.maximum(rel, 0.0)), 0.0) * scale
            qdec_ref[h] = jnp.exp(log_g * (pos + 1.0))
            kdec_ref[h] = jnp.exp(log_g * (n - 1.0 - pos)) * scale

    for h, log_g in enumerate(log_gamma):
        cols = slice(h * HEAD_W, (h + 1) * HEAD_W)
        q = q_ref[:, cols]
        k = k_ref[:, cols]
        v = v_ref[:, cols]
        scores = _dot_nt(q, k) * decay_ref[h]
        intra = _dot(scores.astype(bf16), v)
        state = state_ref[h]
        cross = _dot(q, state.astype(bf16)) * qdec_ref[h]
        k_dec = (k.astype(f32) * kdec_ref[h]).astype(bf16)
        state_ref[h] = math.exp(log_g * n) * state + _dot_tn(k_dec, v)
        o = intra + cross
        o = o * lax.rsqrt(jnp.mean(o * o, axis=-1, keepdims=True) + EPS)
        gate = g_ref[:, cols].astype(f32)
        o_ref[:, cols] = (gate * jax.nn.sigmoid(gate) * o).astype(bf16)


def _retention(proj, batch, seq):
    nb = seq // RET_BLOCK
    spec = lambda c: pl.BlockSpec((RET_BLOCK, SPLIT_W), lambda b, j: (b * nb + j, c))
    return pl.pallas_call(
        _retention_body,
        out_shape=jax.ShapeDtypeStruct((batch * seq, SPLIT_W), bf16),
        grid=(batch, nb),
        in_specs=[spec(0), spec(1), spec(2), spec(3)],
        out_specs=pl.BlockSpec((RET_BLOCK, SPLIT_W), lambda b, j: (b * nb + j, 0)),
        scratch_shapes=[pltpu.VMEM((RET_HEADS, RET_DK, HEAD_W), f32),
                        pltpu.VMEM((RET_HEADS, RET_BLOCK, RET_BLOCK), f32),
                        pltpu.VMEM((RET_HEADS, RET_BLOCK, HEAD_W), f32),
                        pltpu.VMEM((RET_HEADS, RET_BLOCK, HEAD_W), f32)],
        compiler_params=_params("parallel", "arbitrary"),
        name="retention",
    )(proj, proj, proj, proj)


_STRIPS = [slice(s * STRIP, (s + 1) * STRIP) for s in range(TQ // STRIP)]
_fold8 = lambda a: a.reshape(STRIP // 8, 8, TQ)


N_MAPS = 2 * DIFF_HEADS
_SLOPES = [2.0 ** (-8.0 * (h + 1) / DIFF_HEADS) for h in range(DIFF_HEADS)]
_M_OLD, _M_NEW = 0, 1
V_ROWS = HEAD_W + 16
K_AUG = 2 * HEAD_W


_ALL_HEADS = tuple(range(DIFF_HEADS))


def _raw_scores(k_ref, feat_ref, blk, qa_ref, raw_ref, heads=_ALL_HEADS):
    feat = feat_ref[...]
    for h in heads:
        kb = k_ref[pl.ds(pl.multiple_of(blk * TQ, TQ), TQ), h * HEAD_W:(h + 1) * HEAD_W]
        raw_ref[h] = _dot(jnp.concatenate([kb, feat], axis=1), qa_ref[h])


def _advance_max(raw_ref, bias_ref, dist, m_ref, heads=_ALL_HEADS):
    for h in heads:
        for c in range(2):
            m8 = None
            for rows in _STRIPS:
                t = raw_ref[h, rows, c * TQ:(c + 1) * TQ]
                if bias_ref is not None:
                    t = t + bias_ref[h, rows, :]
                g = jnp.max(_fold8(t), axis=0)
                m8 = g if m8 is None else jnp.maximum(m8, g)
            prev = m_ref[_M_NEW, 2 * h + c]
            m_ref[_M_OLD, 2 * h + c] = prev
            m_ref[_M_NEW, 2 * h + c] = jnp.maximum(prev, jnp.max(m8, axis=0, keepdims=True) - _SLOPES[h] * dist)


def _pipeline_step(k_ref, qa_ref, vt_ref, acc_ref, m_ref, cur, nxt):
    raw_c, bias_c, dist_c, blk_c = cur
    raw_n, bias_n, dist_n, blk_n, feat_n = nxt
    for h in _ALL_HEADS:
        _accumulate(raw_c, bias_c, dist_c, m_ref, vt_ref, blk_c, acc_ref, heads=(h,))
        _raw_scores(k_ref, feat_n, blk_n, qa_ref, raw_n, heads=(h,))
        if h > 0:
            _advance_max(raw_n, bias_n, dist_n, m_ref, heads=(h - 1,))
    _advance_max(raw_n, bias_n, dist_n, m_ref, heads=(DIFF_HEADS - 1,))


def _accumulate(raw_ref, bias_ref, dist, m_ref, vt_ref, blk, acc_ref, heads=_ALL_HEADS):
    for h in heads:
        for c in range(2):
            m = 2 * h + c
            m_old, m_new = m_ref[_M_OLD, m], m_ref[_M_NEW, m]
            shift = -_SLOPES[h] * dist - m_new
            p = []
            for rows in _STRIPS:
                t = raw_ref[h, rows, c * TQ:(c + 1) * TQ]
                if bias_ref is not None:
                    t = t + bias_ref[h, rows, :]
                p.append(jnp.exp(t + shift).astype(bf16))
            pv = _dot(vt_ref[h, blk], jnp.concatenate(p, axis=0))
            acc_ref[m] = jnp.exp(m_old - m_new) * acc_ref[m] + pv


def _diffattn_body(q_ref, k_ref, v_ref, lamv_ref, subln_ref, const_ref, o_ref,
                   vt_ref, bias_ref, feat_ref, qa_ref, raw_ref, acc_ref, m_ref):
    qi = pl.program_id(1)

    @pl.when(qi == 0)
    def _():
        key = lax.broadcasted_iota(i32, (TQ, TQ), 0)
        qry = lax.broadcasted_iota(i32, (TQ, TQ), 1)
        dist_in = jnp.abs(qry - key).astype(f32)
        allowed = (key // CHUNK) <= (qry // CHUNK)
        ones_rows = jnp.where(lax.broadcasted_iota(i32, (V_ROWS - HEAD_W, TQ), 0) == 0, 1.0, 0.0).astype(bf16)
        sub = lax.broadcasted_iota(i32, (HEAD_W, 2 * TQ), 0)
        qpos = (lax.broadcasted_iota(i32, (HEAD_W, 2 * TQ), 1) % TQ).astype(f32)
        for h in range(DIFF_HEADS):
            cols = slice(h * HEAD_W, (h + 1) * HEAD_W)
            for j in range(vt_ref.shape[1]):
                vt_ref[h, j, :HEAD_W, :] = v_ref[j * TQ:(j + 1) * TQ, cols].astype(f32).T.astype(bf16)
                vt_ref[h, j, HEAD_W:, :] = ones_rows
            bias_ref[h] = jnp.where(allowed, -_SLOPES[h] * dist_in, NEG_BIG)
            qa_ref[h, HEAD_W:, :] = jnp.where(sub == 0, _SLOPES[h],
                                              jnp.where(sub == 1, -_SLOPES[h] * qpos, 0.0)).astype(bf16)
        lane = lax.broadcasted_iota(i32, (TQ, HEAD_W), 1)
        krow = lax.broadcasted_iota(i32, (TQ, HEAD_W), 0).astype(f32)
        feat_ref[0] = jnp.where(lane == 0, krow, jnp.where(lane == 1, 1.0, 0.0)).astype(bf16)
        feat_ref[1] = jnp.zeros((TQ, HEAD_W), bf16)

    half = lax.broadcasted_iota(i32, (HEAD_W, TQ), 0) < DIFF_DH
    for h in range(DIFF_HEADS):
        q_t = (q_ref[:, h * HEAD_W:(h + 1) * HEAD_W].astype(f32) * (DIFF_DH ** -0.5)).T
        qa_ref[h, :HEAD_W, :] = jnp.concatenate([jnp.where(half, q_t, 0.0), jnp.where(half, 0.0, q_t)],
                                                axis=1).astype(bf16)
    acc_ref[...] = jnp.zeros_like(acc_ref)
    m_ref[_M_NEW] = jnp.full(m_ref.shape[1:], NEG_BIG, f32)

    dist = lambda j: ((qi - j) * TQ).astype(f32)
    zero = jnp.zeros((), f32)
    before, diagonal = feat_ref.at[0], feat_ref.at[1]

    @pl.when(qi == 0)
    def _():
        _raw_scores(k_ref, diagonal, 0, qa_ref, raw_ref.at[0])
        _advance_max(raw_ref.at[0], bias_ref, zero, m_ref)

    @pl.when(qi > 0)
    def _():
        _raw_scores(k_ref, before, 0, qa_ref, raw_ref.at[0])
        _advance_max(raw_ref.at[0], None, dist(0), m_ref)

    def step(j, carry):
        _pipeline_step(k_ref, qa_ref, vt_ref, acc_ref, m_ref,
                       cur=(raw_ref.at[j % 2], None, dist(j), j),
                       nxt=(raw_ref.at[(j + 1) % 2], None, dist(j + 1), j + 1, before))
        return carry

    lax.fori_loop(0, qi - 1, step, 0)

    @pl.when(qi > 0)
    def _():
        _pipeline_step(k_ref, qa_ref, vt_ref, acc_ref, m_ref,
                       cur=(raw_ref.at[(qi - 1) % 2], None, dist(qi - 1), qi - 1),
                       nxt=(raw_ref.at[qi % 2], bias_ref, zero, qi, diagonal))

    _accumulate(raw_ref.at[qi % 2], bias_ref, zero, m_ref, vt_ref, qi, acc_ref)

    lamv = lamv_ref[...]
    lam_init = const_ref[0:1, 0:1]
    lam = (jnp.exp(jnp.sum(lamv[0:1] * lamv[1:2], axis=-1, keepdims=True))
           - jnp.exp(jnp.sum(lamv[2:3] * lamv[3:4], axis=-1, keepdims=True)) + lam_init)
    for h in range(DIFF_HEADS):
        a1, a2 = (acc_ref[2 * h + c, :HEAD_W, :] / acc_ref[2 * h + c, HEAD_W:HEAD_W + 1, :] for c in range(2))
        o = (a1 - lam * a2).T
        o = o * lax.rsqrt(jnp.mean(o * o, axis=-1, keepdims=True) + EPS) * subln_ref[...]
        o_ref[:, h * HEAD_W:(h + 1) * HEAD_W] = (o * (1.0 - lam_init)).astype(bf16)


def _diffattn(proj, lamv, subln, consts, batch, seq):
    nq = seq // TQ
    return pl.pallas_call(
        _diffattn_body,
        out_shape=jax.ShapeDtypeStruct((batch * seq, SPLIT_W), bf16),
        grid=(batch, nq),
        in_specs=[pl.BlockSpec((TQ, SPLIT_W), lambda b, i: (b * nq + i, 4)),
                  pl.BlockSpec((seq, SPLIT_W), lambda b, i: (b, 5)),
                  pl.BlockSpec((seq, SPLIT_W), lambda b, i: (b, 6)),
                  pl.BlockSpec((8, LANES), lambda b, i: (0, 0)),
                  pl.BlockSpec((1, HEAD_W), lambda b, i: (0, 0)),
                  pl.BlockSpec((1, LANES), lambda b, i: (0, 0))],
        out_specs=pl.BlockSpec((TQ, SPLIT_W), lambda b, i: (b * nq + i, 0)),
        scratch_shapes=[pltpu.VMEM((DIFF_HEADS, nq, V_ROWS, TQ), bf16),
                        pltpu.VMEM((DIFF_HEADS, TQ, TQ), f32),
                        pltpu.VMEM((2, TQ, HEAD_W), bf16),
                        pltpu.VMEM((DIFF_HEADS, K_AUG, 2 * TQ), bf16),
                        pltpu.VMEM((2, DIFF_HEADS, TQ, 2 * TQ), f32),
                        pltpu.VMEM((N_MAPS, V_ROWS, TQ), f32),
                        pltpu.VMEM((2, N_MAPS, 1, TQ), f32)],
        compiler_params=_params("parallel", "arbitrary"),
        name="diffattn",
    )(proj, proj, proj, lamv, subln, consts)


def _route_rows(lt):
    lg = [lt[r:r + 1, :] for r in range(N_GROUPS)]
    best, gidx = lg[0], jnp.zeros(lg[0].shape, i32)
    for r in range(1, N_GROUPS):
        upd = lg[r] > best
        gidx = jnp.where(upd, r, gidx)
        best = jnp.where(upd, lg[r], best)
    den = sum(jnp.exp(v - best) for v in lg)
    g_w = 1.0 / den

    def sel(e):
        rows = [lt[8 + 4 * g + e:9 + 4 * g + e, :] for g in range(N_GROUPS)]
        return jnp.where(gidx == 0, rows[0], jnp.where(gidx == 1, rows[1], jnp.where(gidx == 2, rows[2], rows[3])))

    le = [sel(e) for e in range(EXPERTS_PER_GROUP)]
    v1, i1 = le[0], jnp.zeros(le[0].shape, i32)
    for e in range(1, EXPERTS_PER_GROUP):
        upd = le[e] > v1
        i1 = jnp.where(upd, e, i1)
        v1 = jnp.where(upd, le[e], v1)
    rest = [jnp.where(i1 == e, -jnp.inf, le[e]) for e in range(EXPERTS_PER_GROUP)]
    v2, i2 = rest[0], jnp.zeros(le[0].shape, i32)
    for e in range(1, EXPERTS_PER_GROUP):
        upd = rest[e] > v2
        i2 = jnp.where(upd, e, i2)
        v2 = jnp.where(upd, rest[e], v2)
    e2 = jnp.exp(v2 - v1)
    w1 = (1.0 / (1.0 + e2)) * g_w
    w2 = (e2 / (1.0 + e2)) * g_w
    first_low = i1 < i2
    lo = jnp.minimum(i1, i2)
    hi = jnp.maximum(i1, i2)
    pair = jnp.where(lo == 0, hi - 1, jnp.where(lo == 1, hi + 1, 5))
    return gidx * N_PAIRS + pair, jnp.where(first_low, w1, w2), jnp.where(first_low, w2, w1)


def _outproj_body(ret_ref, diff_ref, wo_ref, x_ref, g_ref, wr_ref, rb_ref,
                  xw_ref, route_ref, cnt_ref, run_ref, tri_ref):
    @pl.when(pl.program_id(0) == 0)
    def _():
        run_ref[...] = jnp.zeros_like(run_ref)
        earlier = lax.broadcasted_iota(i32, (TM, TM), 0) < lax.broadcasted_iota(i32, (TM, TM), 1)
        tri_ref[...] = jnp.where(earlier, 1.0, 0.0).astype(bf16)

    x1 = x_ref[...] + _dot(ret_ref[...], wo_ref[:SPLIT_W, :]) + _dot(diff_ref[...], wo_ref[SPLIT_W:, :])
    for c in range(X_CHUNKS):
        xw_ref[c] = x1[:, c * LANES:(c + 1) * LANES]

    h = _rms(x1, g_ref[...])
    h_hi = h.astype(bf16)
    h_lo = (h - h_hi.astype(f32)).astype(bf16)
    r1 = _dot(h_hi, wr_ref[...])
    logits = r1[:, :LANES] + (r1[:, LANES:] + _dot(h_lo, wr_ref[:, :LANES])) + rb_ref[...]
    bucket, w_low, w_high = _route_rows(logits.T)

    sub = lax.broadcasted_iota(i32, (LANES, TM), 0)
    wt = jnp.where(sub == 0, w_low, jnp.where(sub == 1, w_high, 0.0))
    xw_ref[X_CHUNKS] = wt.T

    onehot = lax.broadcasted_iota(i32, (BUCKET_ROWS, TM), 0) == bucket
    oh_b = jnp.where(onehot, 1.0, 0.0).astype(bf16)
    before = _dot(oh_b, tri_ref[...])
    run = run_ref[...]
    oh_f = jnp.where(onehot, 1.0, 0.0)
    rank = jnp.sum(oh_f * (before + run[:, 0:1]), axis=0, keepdims=True).astype(i32)
    sub8 = lax.broadcasted_iota(i32, (8, TM), 0)
    route_ref[0] = jnp.where(sub8 == 0, bucket, jnp.where(sub8 == 1, rank, 0))
    run = run + jnp.sum(oh_f, axis=1, keepdims=True)
    run_ref[...] = run
    cnt_ref[...] = run


def _outproj(ret, diff, wo, x, g, wr, rb, first_block):
    t = ret.shape[0]
    nt = t // TM
    return pl.pallas_call(
        _outproj_body,
        out_shape=(jax.ShapeDtypeStruct((ROW_CHUNKS, t, LANES), f32),
                   jax.ShapeDtypeStruct((nt, 8, TM), i32),
                   jax.ShapeDtypeStruct((BUCKET_ROWS, LANES), f32)),
        grid=(nt,),
        in_specs=[pl.BlockSpec((TM, SPLIT_W), lambda i: (i, 0)),
                  pl.BlockSpec((TM, SPLIT_W), lambda i: (i, 0)),
                  pl.BlockSpec((D_MODEL, D_MODEL), lambda i: (0, 0)),
                  pl.BlockSpec((TM, D_MODEL), lambda i: (i + first_block, 0)),
                  pl.BlockSpec((1, D_MODEL), lambda i: (0, 0)),
                  pl.BlockSpec((D_MODEL, 2 * LANES), lambda i: (0, 0)),
                  pl.BlockSpec((1, LANES), lambda i: (0, 0))],
        out_specs=(pl.BlockSpec((ROW_CHUNKS, TM, LANES), lambda i: (0, i, 0)),
                   pl.BlockSpec((1, 8, TM), lambda i: (i, 0, 0)),
                   pl.BlockSpec((BUCKET_ROWS, LANES), lambda i: (0, 0))),
        scratch_shapes=[pltpu.VMEM((BUCKET_ROWS, LANES), f32),
                        pltpu.VMEM((TM, TM), bf16)],
        compiler_params=_params("arbitrary"),
        name="outproj_router",
    )(ret, diff, wo, x, g, wr, rb)


def _sc_mesh():
    return plsc.VectorSubcoreMesh(core_axis_name="core", subcore_axis_name="subcore")


def _chunk_rows(pos, chunks, rows_per_chunk):
    return (pos[None, :] + jnp.arange(chunks, dtype=i32)[:, None] * rows_per_chunk).reshape(1, -1)


def _sc_dispatch(pos, xw, rows_out):
    chunks, t, _ = xw.shape
    n = chunks * t

    @functools.partial(pl.kernel, out_type=jax.ShapeDtypeStruct((chunks * rows_out, LANES), xw.dtype),
                       mesh=_sc_mesh(), scratch_types=[])
    def run(x_hbm, i_hbm, o_hbm):
        def body(x_vmem, i_vmem):
            pltpu.sync_copy(x_vmem, o_hbm.at[i_vmem.at[0]])

        pltpu.emit_pipeline(
            body,
            grid=(n // SC_WINDOW,),
            in_specs=[pl.BlockSpec((SC_WINDOW, LANES), lambda i: (i, 0)),
                      pl.BlockSpec((1, SC_WINDOW), lambda i: (0, i))],
            out_specs=[],
            core_axis_name=("core", "subcore"),
            dimension_semantics=(pltpu.PARALLEL,),
        )(x_hbm, i_hbm)

    out = run(xw.reshape(n, LANES), _chunk_rows(pos, chunks, rows_out))
    return out.reshape(chunks, rows_out, LANES)


def _sc_combine(pos, ys, t):
    chunks, rows_in, _ = ys.shape
    n = chunks * t

    @functools.partial(pl.kernel, out_type=jax.ShapeDtypeStruct((n, LANES), ys.dtype),
                       mesh=_sc_mesh(), scratch_types=[])
    def run(y_hbm, i_hbm, o_hbm):
        def body(i_vmem, o_vmem):
            pltpu.sync_copy(y_hbm.at[i_vmem.at[0]], o_vmem)

        pltpu.emit_pipeline(
            body,
            grid=(n // SC_WINDOW,),
            in_specs=[pl.BlockSpec((1, SC_WINDOW), lambda i: (0, i))],
            out_specs=[pl.BlockSpec((SC_WINDOW, LANES), lambda i: (i, 0))],
            core_axis_name=("core", "subcore"),
            dimension_semantics=(pltpu.PARALLEL,),
        )(i_hbm, o_hbm)

    out = run(ys.reshape(chunks * rows_in, LANES), _chunk_rows(pos, chunks, rows_in))
    return out.reshape(chunks, t, LANES)


def _moe_body(e_lo_ref, e_hi_ref, valid_ref, xs_ref, g_ref, wg1, wu1, wd1, wg2, wu2, wd2, o_ref):
    nv = valid_ref[pl.program_id(0)]

    @pl.when(nv > 0)
    def _():
        ok = lax.broadcasted_iota(i32, (TM_MOE, 1), 0) < nv
        x = jnp.where(ok, jnp.concatenate([xs_ref[c] for c in range(X_CHUNKS)], axis=1), 0.0)
        w_lo = jnp.where(ok, xs_ref[X_CHUNKS, :, 0:1], 0.0)
        w_hi = jnp.where(ok, xs_ref[X_CHUNKS, :, 1:2], 0.0)
        h = _rms(x, g_ref[...]).astype(bf16)
        cast = lambda w_ref: w_ref[...].astype(bf16)
        hid1 = jax.nn.silu(_dot(h, cast(wg1))) * _dot(h, cast(wu1)) * w_lo
        hid2 = jax.nn.silu(_dot(h, cast(wg2))) * _dot(h, cast(wu2)) * w_hi
        y = x + (_dot(hid1.astype(bf16), cast(wd1)) + _dot(hid2.astype(bf16), cast(wd2)))
        for c in range(X_CHUNKS):
            o_ref[c] = y[:, c * LANES:(c + 1) * LANES]

    @pl.when(nv == 0)
    def _():
        o_ref[...] = jnp.zeros_like(o_ref)


def _moe(layer, e_lo, e_hi, valid, xs, g, wg, wu, wd):
    n_tiles = xs.shape[1] // TM_MOE
    up = lambda sel: pl.BlockSpec((None, None, D_MODEL, D_EXPERT),
                                  lambda i, lo, hi, nv: (layer, (lo, hi)[sel][i], 0, 0))
    down = lambda sel: pl.BlockSpec((None, None, D_EXPERT, D_MODEL),
                                    lambda i, lo, hi, nv: (layer, (lo, hi)[sel][i], 0, 0))
    return pl.pallas_call(
        _moe_body,
        out_shape=jax.ShapeDtypeStruct((X_CHUNKS, xs.shape[1], LANES), f32),
        grid_spec=pltpu.PrefetchScalarGridSpec(
            num_scalar_prefetch=3,
            grid=(n_tiles,),
            in_specs=[pl.BlockSpec((ROW_CHUNKS, TM_MOE, LANES), lambda i, lo, hi, nv: (0, i, 0)),
                      pl.BlockSpec((1, D_MODEL), lambda i, lo, hi, nv: (0, 0)),
                      up(0), up(0), down(0), up(1), up(1), down(1)],
            out_specs=pl.BlockSpec((X_CHUNKS, TM_MOE, LANES), lambda i, lo, hi, nv: (0, i, 0))),
        compiler_params=_params("arbitrary"),
        name="moe",
    )(e_lo, e_hi, valid, xs, g, wg, wu, wd, wg, wu, wd)


def _sort_plan(route, counts_f, n_tiles):
    bucket = route[:, 0, :].reshape(-1)
    rank = route[:, 1, :].reshape(-1)
    counts = counts_f[:N_BUCKETS, 0].astype(i32)
    tiles_per = (counts + TM_MOE - 1) // TM_MOE
    tile_end = jnp.cumsum(tiles_per)
    tile_start = tile_end - tiles_per
    pos = (tile_start * TM_MOE)[bucket] + rank
    tile = jnp.arange(n_tiles, dtype=i32)
    used = tile < tile_end[-1]
    owner = lambda tl: jnp.minimum(jnp.sum((tl[:, None] >= tile_end[None, :]).astype(i32), axis=1), N_BUCKETS - 1)
    tb = owner(jnp.where(used, tile, tile_end[-1] - 1))
    valid = jnp.where(used, jnp.clip(counts[tb] - (tile - tile_start[tb]) * TM_MOE, 0, TM_MOE), 0).astype(i32)
    pair_lo = jnp.array([0, 0, 0, 1, 1, 2], i32)
    pair_hi = jnp.array([1, 2, 3, 2, 3, 3], i32)
    grp = tb // N_PAIRS
    e_lo = grp * EXPERTS_PER_GROUP + pair_lo[tb % N_PAIRS]
    e_hi = grp * EXPERTS_PER_GROUP + pair_hi[tb % N_PAIRS]
    return pos.astype(i32), e_lo, e_hi, valid


def _ple_body(x_ref, p_ref, g_ref, wg_ref, wp_ref, gf_ref, *rest, final):
    o_ref = rest[-1]
    x = jnp.concatenate([x_ref[c] for c in range(X_CHUNKS)], axis=1)
    gate = jax.nn.sigmoid(_dot(_rms(x, g_ref[...]).astype(bf16), wg_ref[...]))
    y = x + gate * _dot(p_ref[...].astype(bf16), wp_ref[...])
    o_ref[...] = _rms(y, gf_ref[...]) if final else y


def _ple(x, p, g, wg, wp, gf, p_first_block, final, out_rows=None, out_first_block=0, out_buffer=None):
    t = x.shape[1]
    in_specs = [pl.BlockSpec((X_CHUNKS, TM, LANES), lambda i: (0, i, 0)),
                pl.BlockSpec((TM, D_PLE), lambda i: (p_first_block + i, 0)),
                pl.BlockSpec((1, D_MODEL), lambda i: (0, 0)),
                pl.BlockSpec((D_MODEL, D_MODEL), lambda i: (0, 0)),
                pl.BlockSpec((D_PLE, D_MODEL), lambda i: (0, 0)),
                pl.BlockSpec((1, D_MODEL), lambda i: (0, 0))]
    args = [x, p, g, wg, wp, gf]
    aliases = {}
    if out_buffer is not None:
        in_specs.append(pl.BlockSpec(memory_space=pl.ANY))
        aliases = {len(args): 0}
        args.append(out_buffer)
    return pl.pallas_call(
        functools.partial(_ple_body, final=final),
        out_shape=jax.ShapeDtypeStruct((out_rows or t, D_MODEL), f32),
        grid=(t // TM,),
        in_specs=in_specs,
        out_specs=pl.BlockSpec((TM, D_MODEL), lambda i: (i + out_first_block, 0)),
        input_output_aliases=aliases,
        compiler_params=_params("parallel"),
        name="ple_final" if final else "ple",
    )(*args)


def _router_weights(w_rg, b_rg, w_re, b_re):
    w = jnp.zeros((D_MODEL, LANES), f32)
    w = w.at[:, :N_GROUPS].set(w_rg).at[:, 8:8 + N_EXPERTS].set(w_re)
    hi = w.astype(bf16)
    lo = (w - hi.astype(f32)).astype(bf16)
    b = jnp.zeros((1, LANES), f32).at[0, :N_GROUPS].set(b_rg).at[0, 8:8 + N_EXPERTS].set(b_re)
    return jnp.concatenate([hi, lo], axis=1), b


def kernel(x, p, norm_mix, w_in, lam_q1, lam_k1, lam_q2, lam_k2, diff_subln, w_out, norm_ffn,
           w_router_group, b_router_group, w_router_expert, b_router_expert, w_gate, w_up, w_down,
           norm_ple, w_ple_gate, w_ple_proj, norm_final):
    batch, seq, _ = x.shape
    t = batch * seq
    streams = STREAMS if batch % STREAMS == 0 else 1
    sb = batch // streams
    st = sb * seq
    n_tiles = st // TM_MOE + N_BUCKETS
    p_rows = p.reshape(DEPTH * t, D_PLE)
    row = lambda v: v.reshape(1, -1).astype(f32)
    cur = [(x.reshape(t, D_MODEL), s * st // TM) for s in range(streams)]
    out = None
    for i in range(DEPTH):
        last = i == DEPTH - 1
        lam_init = 0.8 - 0.6 * math.exp(-0.3 * i)
        lamv = jnp.zeros((8, LANES), f32)
        for r, v in enumerate((lam_q1[i], lam_k1[i], lam_q2[i], lam_k2[i])):
            lamv = lamv.at[r, :DIFF_DH].set(v.astype(f32))
        consts = jnp.full((1, LANES), lam_init, f32)
        wr, rb = _router_weights(w_router_group[i], b_router_group[i], w_router_expert[i], b_router_expert[i])
        w_in_i, w_out_i = w_in[i].astype(bf16), w_out[i].astype(bf16)
        w_pg, w_pp = w_ple_gate[i].astype(bf16), w_ple_proj[i].astype(bf16)
        nxt = []
        for s, (xs_rows, first_block) in enumerate(cur):
            proj = _inproj(xs_rows, row(norm_mix[i]), w_in_i, st, first_block)
            ret = _retention(proj, sb, seq)
            diff = _diffattn(proj, lamv, row(diff_subln[i]), consts, sb, seq)
            xw, route, counts = _outproj(ret, diff, w_out_i, xs_rows, row(norm_ffn[i]), wr, rb, first_block)
            pos, e_lo, e_hi, valid = _sort_plan(route, counts, n_tiles)
            xs = _sc_dispatch(pos, xw, n_tiles * TM_MOE)
            ys = _moe(i, e_lo, e_hi, valid, xs, row(norm_ffn[i]), w_gate, w_up, w_down)
            x2 = _sc_combine(pos, ys, st)
            p_first = (i * t + s * st) // TM
            if last:
                out = _ple(x2, p_rows, row(norm_ple[i]), w_pg, w_pp, row(norm_final), p_first, final=True,
                           out_rows=t, out_first_block=s * st // TM, out_buffer=out)
            else:
                nxt.append((_ple(x2, p_rows, row(norm_ple[i]), w_pg, w_pp, row(norm_final), p_first, final=False), 0))
        cur = nxt
    return out.reshape(batch, seq, D_MODEL)
```

```python
import functools
import math

import jax
import jax.numpy as jnp
from jax import lax
from jax.experimental import pallas as pl
from jax.experimental.pallas import tpu as pltpu
from jax.experimental.pallas import tpu_sc as plsc

f32 = jnp.float32
bf16 = jnp.bfloat16
i32 = jnp.int32

D_MODEL = 1024
DEPTH = 4
CHUNK = 64
D_PLE = 256
EPS = 1e-6
RET_HEADS = 4
RET_DK = 128
DIFF_HEADS = 4
DIFF_DH = 64
HEAD_W = 128
SPLIT_W = 512
IN_COLS = 7 * SPLIT_W
N_GROUPS = 4
EXPERTS_PER_GROUP = 4
N_EXPERTS = 16
D_EXPERT = 512
N_PAIRS = 6
N_BUCKETS = N_GROUPS * N_PAIRS
BUCKET_ROWS = 32

LANES = 128
HALF_W = D_MODEL // 2
X_CHUNKS = HALF_W // LANES
ROW_CHUNKS = X_CHUNKS + 1

TM = 512
RET_BLOCK = 256
TQ = 256
STRIP = 64
TM_MOE = 512
SC_WINDOW = 128
VMEM_LIMIT = 56 * 1024 * 1024
NEG_BIG = -1e30


def _rms(x, g):
    return x * lax.rsqrt(jnp.mean(x * x, axis=-1, keepdims=True) + EPS) * g


def _dot(a, b):
    return jnp.dot(a, b, preferred_element_type=f32)


def _dot_nt(a, b):
    return lax.dot_general(a, b, (((1,), (1,)), ((), ())), preferred_element_type=f32)


def _dot_tn(a, b):
    return lax.dot_general(a, b, (((0,), (0,)), ((), ())), preferred_element_type=f32)


def _pack_bf16_pairs(v):
    as_bits = lambda a: pltpu.bitcast(a.astype(bf16).astype(f32), jnp.uint32)
    word = (as_bits(v[:, :HALF_W]) & jnp.uint32(0xFFFF0000)) | (as_bits(v[:, HALF_W:]) >> jnp.uint32(16))
    return pltpu.bitcast(word, f32)


def _unpack_bf16_pairs(words):
    bits = pltpu.bitcast(words, jnp.uint32)
    high = pltpu.bitcast(bits & jnp.uint32(0xFFFF0000), f32)
    low = pltpu.bitcast(bits << jnp.uint32(16), f32)
    return jnp.concatenate([high, low], axis=1)


def _params(*sem):
    return pltpu.CompilerParams(dimension_semantics=sem, vmem_limit_bytes=VMEM_LIMIT)


def _inproj_body(x_ref, g_ref, w_ref, o_ref):
    h = _rms(x_ref[...], g_ref[...]).astype(bf16)
    for n in range(IN_COLS // SPLIT_W):
        cols = slice(n * SPLIT_W, (n + 1) * SPLIT_W)
        o_ref[:, cols] = _dot(h, w_ref[:, cols]).astype(bf16)


def _inproj(x, g, w):
    t = x.shape[0]
    return pl.pallas_call(
        _inproj_body,
        out_shape=jax.ShapeDtypeStruct((t, IN_COLS), bf16),
        grid=(t // TM,),
        in_specs=[pl.BlockSpec((TM, D_MODEL), lambda i: (i, 0)),
                  pl.BlockSpec((1, D_MODEL), lambda i: (0, 0)),
                  pl.BlockSpec((D_MODEL, IN_COLS), lambda i: (0, 0))],
        out_specs=pl.BlockSpec((TM, IN_COLS), lambda i: (i, 0)),
        compiler_params=_params("parallel"),
        name="inproj",
    )(x, g, w)


def _retention_body(q_ref, k_ref, v_ref, g_ref, o_ref, state_ref, decay_ref, qdec_ref, kdec_ref):
    n = RET_BLOCK
    scale = RET_DK ** -0.5
    log_gamma = [math.log1p(-(2.0 ** (-5.0 - h))) for h in range(RET_HEADS)]

    @pl.when(pl.program_id(1) == 0)
    def _():
        state_ref[...] = jnp.zeros_like(state_ref)
        rel = (lax.broadcasted_iota(i32, (n, n), 0) - lax.broadcasted_iota(i32, (n, n), 1)).astype(f32)
        pos = lax.broadcasted_iota(i32, (n, HEAD_W), 0).astype(f32)
        for h, log_g in enumerate(log_gamma):
            decay_ref[h] = jnp.where(rel >= 0, jnp.exp(log_g * jnp.maximum(rel, 0.0)), 0.0) * scale
            qdec_ref[h] = jnp.exp(log_g * (pos + 1.0))
            kdec_ref[h] = jnp.exp(log_g * (n - 1.0 - pos)) * scale

    for h, log_g in enumerate(log_gamma):
        cols = slice(h * HEAD_W, (h + 1) * HEAD_W)
        q = q_ref[:, cols]
        k = k_ref[:, cols]
        v = v_ref[:, cols]
        scores = _dot_nt(q, k) * decay_ref[h]
        intra = _dot(scores.astype(bf16), v)
        state = state_ref[h]
        cross = _dot(q, state.astype(bf16)) * qdec_ref[h]
        k_dec = (k.astype(f32) * kdec_ref[h]).astype(bf16)
        state_ref[h] = math.exp(log_g * n) * state + _dot_tn(k_dec, v)
        o = intra + cross
        o = o * lax.rsqrt(jnp.mean(o * o, axis=-1, keepdims=True) + EPS)
        gate = g_ref[:, cols].astype(f32)
        o_ref[:, cols] = (gate * jax.nn.sigmoid(gate) * o).astype(bf16)


def _retention(proj, batch, seq):
    nb = seq // RET_BLOCK
    spec = lambda c: pl.BlockSpec((RET_BLOCK, SPLIT_W), lambda b, j: (b * nb + j, c))
    return pl.pallas_call(
        _retention_body,
        out_shape=jax.ShapeDtypeStruct((batch * seq, SPLIT_W), bf16),
        grid=(batch, nb),
        in_specs=[spec(0), spec(1), spec(2), spec(3)],
        out_specs=pl.BlockSpec((RET_BLOCK, SPLIT_W), lambda b, j: (b * nb + j, 0)),
        scratch_shapes=[pltpu.VMEM((RET_HEADS, RET_DK, HEAD_W), f32),
                        pltpu.VMEM((RET_HEADS, RET_BLOCK, RET_BLOCK), f32),
                        pltpu.VMEM((RET_HEADS, RET_BLOCK, HEAD_W), f32),
                        pltpu.VMEM((RET_HEADS, RET_BLOCK, HEAD_W), f32)],
        compiler_params=_params("parallel", "arbitrary"),
        name="retention",
    )(proj, proj, proj, proj)


_STRIPS = [slice(s * STRIP, (s + 1) * STRIP) for s in range(TQ // STRIP)]
_fold8 = lambda a: a.reshape(STRIP // 8, 8, TQ)


N_MAPS = 2 * DIFF_HEADS
_SLOPES = [2.0 ** (-8.0 * (h + 1) / DIFF_HEADS) for h in range(DIFF_HEADS)]
_M_OLD, _M_NEW = 0, 1
V_ROWS = HEAD_W + 16
K_AUG = 2 * HEAD_W


_ALL_HEADS = tuple(range(DIFF_HEADS))


def _raw_scores(k_ref, feat_ref, blk, qa_ref, raw_ref, heads=_ALL_HEADS):
    feat = feat_ref[...]
    for h in heads:
        kb = k_ref[pl.ds(pl.multiple_of(blk * TQ, TQ), TQ), h * HEAD_W:(h + 1) * HEAD_W]
        raw_ref[h] = _dot(jnp.concatenate([kb, feat], axis=1), qa_ref[h])


def _advance_max(raw_ref, bias_ref, dist, m_ref, heads=_ALL_HEADS):
    for h in heads:
        for c in range(2):
            m8 = None
            for rows in _STRIPS:
                t = raw_ref[h, rows, c * TQ:(c + 1) * TQ]
                if bias_ref is not None:
                    t = t + bias_ref[h, rows, :]
                g = jnp.max(_fold8(t), axis=0)
                m8 = g if m8 is None else jnp.maximum(m8, g)
            prev = m_ref[_M_NEW, 2 * h + c]
            m_ref[_M_OLD, 2 * h + c] = prev
            m_ref[_M_NEW, 2 * h + c] = jnp.maximum(prev, jnp.max(m8, axis=0, keepdims=True) - _SLOPES[h] * dist)


def _pipeline_step(k_ref, qa_ref, vt_ref, acc_ref, m_ref, cur, nxt):
    raw_c, bias_c, dist_c, blk_c = cur
    raw_n, bias_n, dist_n, blk_n, feat_n = nxt
    for h in _ALL_HEADS:
        _accumulate(raw_c, bias_c, dist_c, m_ref, vt_ref, blk_c, acc_ref, heads=(h,))
        _raw_scores(k_ref, feat_n, blk_n, qa_ref, raw_n, heads=(h,))
        if h > 0:
            _advance_max(raw_n, bias_n, dist_n, m_ref, heads=(h - 1,))
    _advance_max(raw_n, bias_n, dist_n, m_ref, heads=(DIFF_HEADS - 1,))


def _accumulate(raw_ref, bias_ref, dist, m_ref, vt_ref, blk, acc_ref, heads=_ALL_HEADS):
    for h in heads:
        for c in range(2):
            m = 2 * h + c
            m_old, m_new = m_ref[_M_OLD, m], m_ref[_M_NEW, m]
            shift = -_SLOPES[h] * dist - m_new
            p = []
            for rows in _STRIPS:
                t = raw_ref[h, rows, c * TQ:(c + 1) * TQ]
                if bias_ref is not None:
                    t = t + bias_ref[h, rows, :]
                p.append(jnp.exp(t + shift).astype(bf16))
            pv = _dot(vt_ref[h, blk], jnp.concatenate(p, axis=0))
            acc_ref[m] = jnp.exp(m_old - m_new) * acc_ref[m] + pv


def _diffattn_body(q_ref, k_ref, v_ref, lamv_ref, subln_ref, const_ref, o_ref,
                   vt_ref, bias_ref, feat_ref, qa_ref, raw_ref, acc_ref, m_ref):
    qi = pl.program_id(1)

    @pl.when(qi == 0)
    def _():
        key = lax.broadcasted_iota(i32, (TQ, TQ), 0)
        qry = lax.broadcasted_iota(i32, (TQ, TQ), 1)
        dist_in = jnp.abs(qry - key).astype(f32)
        allowed = (key // CHUNK) <= (qry // CHUNK)
        ones_rows = jnp.where(lax.broadcasted_iota(i32, (V_ROWS - HEAD_W, TQ), 0) == 0, 1.0, 0.0).astype(bf16)
        sub = lax.broadcasted_iota(i32, (HEAD_W, 2 * TQ), 0)
        qpos = (lax.broadcasted_iota(i32, (HEAD_W, 2 * TQ), 1) % TQ).astype(f32)
        for h in range(DIFF_HEADS):
            cols = slice(h * HEAD_W, (h + 1) * HEAD_W)
            for j in range(vt_ref.shape[1]):
                vt_ref[h, j, :HEAD_W, :] = v_ref[j * TQ:(j + 1) * TQ, cols].astype(f32).T.astype(bf16)
                vt_ref[h, j, HEAD_W:, :] = ones_rows
            bias_ref[h] = jnp.where(allowed, -_SLOPES[h] * dist_in, NEG_BIG)
            qa_ref[h, HEAD_W:, :] = jnp.where(sub == 0, _SLOPES[h],
                                              jnp.where(sub == 1, -_SLOPES[h] * qpos, 0.0)).astype(bf16)
        lane = lax.broadcasted_iota(i32, (TQ, HEAD_W), 1)
        krow = lax.broadcasted_iota(i32, (TQ, HEAD_W), 0).astype(f32)
        feat_ref[0] = jnp.where(lane == 0, krow, jnp.where(lane == 1, 1.0, 0.0)).astype(bf16)
        feat_ref[1] = jnp.zeros((TQ, HEAD_W), bf16)

    half = lax.broadcasted_iota(i32, (HEAD_W, TQ), 0) < DIFF_DH
    for h in range(DIFF_HEADS):
        q_t = (q_ref[:, h * HEAD_W:(h + 1) * HEAD_W].astype(f32) * (DIFF_DH ** -0.5)).T
        qa_ref[h, :HEAD_W, :] = jnp.concatenate([jnp.where(half, q_t, 0.0), jnp.where(half, 0.0, q_t)],
                                                axis=1).astype(bf16)
    acc_ref[...] = jnp.zeros_like(acc_ref)
    m_ref[_M_NEW] = jnp.full(m_ref.shape[1:], NEG_BIG, f32)

    dist = lambda j: ((qi - j) * TQ).astype(f32)
    zero = jnp.zeros((), f32)
    before, diagonal = feat_ref.at[0], feat_ref.at[1]

    @pl.when(qi == 0)
    def _():
        _raw_scores(k_ref, diagonal, 0, qa_ref, raw_ref.at[0])
        _advance_max(raw_ref.at[0], bias_ref, zero, m_ref)

    @pl.when(qi > 0)
    def _():
        _raw_scores(k_ref, before, 0, qa_ref, raw_ref.at[0])
        _advance_max(raw_ref.at[0], None, dist(0), m_ref)

    def step(j, carry):
        _pipeline_step(k_ref, qa_ref, vt_ref, acc_ref, m_ref,
                       cur=(raw_ref.at[j % 2], None, dist(j), j),
                       nxt=(raw_ref.at[(j + 1) % 2], None, dist(j + 1), j + 1, before))
        return carry

    lax.fori_loop(0, qi - 1, step, 0)

    @pl.when(qi > 0)
    def _():
        _pipeline_step(k_ref, qa_ref, vt_ref, acc_ref, m_ref,
                       cur=(raw_ref.at[(qi - 1) % 2], None, dist(qi - 1), qi - 1),
                       nxt=(raw_ref.at[qi % 2], bias_ref, zero, qi, diagonal))

    _accumulate(raw_ref.at[qi % 2], bias_ref, zero, m_ref, vt_ref, qi, acc_ref)

    lamv = lamv_ref[...]
    lam_init = const_ref[0:1, 0:1]
    lam = (jnp.exp(jnp.sum(lamv[0:1] * lamv[1:2], axis=-1, keepdims=True))
           - jnp.exp(jnp.sum(lamv[2:3] * lamv[3:4], axis=-1, keepdims=True)) + lam_init)
    for h in range(DIFF_HEADS):
        a1, a2 = (acc_ref[2 * h + c, :HEAD_W, :] / acc_ref[2 * h + c, HEAD_W:HEAD_W + 1, :] for c in range(2))
        o = (a1 - lam * a2).T
        o = o * lax.rsqrt(jnp.mean(o * o, axis=-1, keepdims=True) + EPS) * subln_ref[...]
        o_ref[:, h * HEAD_W:(h + 1) * HEAD_W] = (o * (1.0 - lam_init)).astype(bf16)


def _diffattn(proj, lamv, subln, consts, batch, seq):
    nq = seq // TQ
    return pl.pallas_call(
        _diffattn_body,
        out_shape=jax.ShapeDtypeStruct((batch * seq, SPLIT_W), bf16),
        grid=(batch, nq),
        in_specs=[pl.BlockSpec((TQ, SPLIT_W), lambda b, i: (b * nq + i, 4)),
                  pl.BlockSpec((seq, SPLIT_W), lambda b, i: (b, 5)),
                  pl.BlockSpec((seq, SPLIT_W), lambda b, i: (b, 6)),
                  pl.BlockSpec((8, LANES), lambda b, i: (0, 0)),
                  pl.BlockSpec((1, HEAD_W), lambda b, i: (0, 0)),
                  pl.BlockSpec((1, LANES), lambda b, i: (0, 0))],
        out_specs=pl.BlockSpec((TQ, SPLIT_W), lambda b, i: (b * nq + i, 0)),
        scratch_shapes=[pltpu.VMEM((DIFF_HEADS, nq, V_ROWS, TQ), bf16),
                        pltpu.VMEM((DIFF_HEADS, TQ, TQ), f32),
                        pltpu.VMEM((2, TQ, HEAD_W), bf16),
                        pltpu.VMEM((DIFF_HEADS, K_AUG, 2 * TQ), bf16),
                        pltpu.VMEM((2, DIFF_HEADS, TQ, 2 * TQ), f32),
                        pltpu.VMEM((N_MAPS, V_ROWS, TQ), f32),
                        pltpu.VMEM((2, N_MAPS, 1, TQ), f32)],
        compiler_params=_params("parallel", "arbitrary"),
        name="diffattn",
    )(proj, proj, proj, lamv, subln, consts)


def _route_rows(lt):
    lg = [lt[r:r + 1, :] for r in range(N_GROUPS)]
    best, gidx = lg[0], jnp.zeros(lg[0].shape, i32)
    for r in range(1, N_GROUPS):
        upd = lg[r] > best
        gidx = jnp.where(upd, r, gidx)
        best = jnp.where(upd, lg[r], best)
    den = sum(jnp.exp(v - best) for v in lg)
    g_w = 1.0 / den

    def sel(e):
        rows = [lt[8 + 4 * g + e:9 + 4 * g + e, :] for g in range(N_GROUPS)]
        return jnp.where(gidx == 0, rows[0], jnp.where(gidx == 1, rows[1], jnp.where(gidx == 2, rows[2], rows[3])))

    le = [sel(e) for e in range(EXPERTS_PER_GROUP)]
    v1, i1 = le[0], jnp.zeros(le[0].shape, i32)
    for e in range(1, EXPERTS_PER_GROUP):
        upd = le[e] > v1
        i1 = jnp.where(upd, e, i1)
        v1 = jnp.where(upd, le[e], v1)
    rest = [jnp.where(i1 == e, -jnp.inf, le[e]) for e in range(EXPERTS_PER_GROUP)]
    v2, i2 = rest[0], jnp.zeros(le[0].shape, i32)
    for e in range(1, EXPERTS_PER_GROUP):
        upd = rest[e] > v2
        i2 = jnp.where(upd, e, i2)
        v2 = jnp.where(upd, rest[e], v2)
    e2 = jnp.exp(v2 - v1)
    w1 = (1.0 / (1.0 + e2)) * g_w
    w2 = (e2 / (1.0 + e2)) * g_w
    first_low = i1 < i2
    lo = jnp.minimum(i1, i2)
    hi = jnp.maximum(i1, i2)
    pair = jnp.where(lo == 0, hi - 1, jnp.where(lo == 1, hi + 1, 5))
    return gidx * N_PAIRS + pair, jnp.where(first_low, w1, w2), jnp.where(first_low, w2, w1)


def _outproj_body(ret_ref, diff_ref, wo_ref, x_ref, g_ref, wr_ref, rb_ref,
                  x1_ref, hw_ref, route_ref, cnt_ref, run_ref, tri_ref):
    @pl.when(pl.program_id(0) == 0)
    def _():
        run_ref[...] = jnp.zeros_like(run_ref)
        earlier = lax.broadcasted_iota(i32, (TM, TM), 0) < lax.broadcasted_iota(i32, (TM, TM), 1)
        tri_ref[...] = jnp.where(earlier, 1.0, 0.0).astype(bf16)

    x1 = x_ref[...] + _dot(ret_ref[...], wo_ref[:SPLIT_W, :]) + _dot(diff_ref[...], wo_ref[SPLIT_W:, :])
    x1_ref[...] = x1

    h = _rms(x1, g_ref[...])
    packed = _pack_bf16_pairs(h)
    for c in range(X_CHUNKS):
        hw_ref[c] = packed[:, c * LANES:(c + 1) * LANES]

    h_hi = h.astype(bf16)
    h_lo = (h - h_hi.astype(f32)).astype(bf16)
    r1 = _dot(h_hi, wr_ref[...])
    logits = r1[:, :LANES] + (r1[:, LANES:] + _dot(h_lo, wr_ref[:, :LANES])) + rb_ref[...]
    bucket, w_low, w_high = _route_rows(logits.T)

    sub = lax.broadcasted_iota(i32, (LANES, TM), 0)
    wt = jnp.where(sub == 0, w_low, jnp.where(sub == 1, w_high, 0.0))
    hw_ref[X_CHUNKS] = wt.T

    onehot = lax.broadcasted_iota(i32, (BUCKET_ROWS, TM), 0) == bucket
    oh_b = jnp.where(onehot, 1.0, 0.0).astype(bf16)
    before = _dot(oh_b, tri_ref[...])
    run = run_ref[...]
    oh_f = jnp.where(onehot, 1.0, 0.0)
    rank = jnp.sum(oh_f * (before + run[:, 0:1]), axis=0, keepdims=True).astype(i32)
    sub8 = lax.broadcasted_iota(i32, (8, TM), 0)
    route_ref[0] = jnp.where(sub8 == 0, bucket, jnp.where(sub8 == 1, rank, 0))
    run = run + jnp.sum(oh_f, axis=1, keepdims=True)
    run_ref[...] = run
    cnt_ref[...] = run


def _outproj(ret, diff, wo, x, g, wr, rb):
    t = x.shape[0]
    nt = t // TM
    return pl.pallas_call(
        _outproj_body,
        out_shape=(jax.ShapeDtypeStruct((t, D_MODEL), f32),
                   jax.ShapeDtypeStruct((ROW_CHUNKS, t, LANES), f32),
                   jax.ShapeDtypeStruct((nt, 8, TM), i32),
                   jax.ShapeDtypeStruct((BUCKET_ROWS, LANES), f32)),
        grid=(nt,),
        in_specs=[pl.BlockSpec((TM, SPLIT_W), lambda i: (i, 0)),
                  pl.BlockSpec((TM, SPLIT_W), lambda i: (i, 0)),
                  pl.BlockSpec((D_MODEL, D_MODEL), lambda i: (0, 0)),
                  pl.BlockSpec((TM, D_MODEL), lambda i: (i, 0)),
                  pl.BlockSpec((1, D_MODEL), lambda i: (0, 0)),
                  pl.BlockSpec((D_MODEL, 2 * LANES), lambda i: (0, 0)),
                  pl.BlockSpec((1, LANES), lambda i: (0, 0))],
        out_specs=(pl.BlockSpec((TM, D_MODEL), lambda i: (i, 0)),
                   pl.BlockSpec((ROW_CHUNKS, TM, LANES), lambda i: (0, i, 0)),
                   pl.BlockSpec((1, 8, TM), lambda i: (i, 0, 0)),
                   pl.BlockSpec((BUCKET_ROWS, LANES), lambda i: (0, 0))),
        scratch_shapes=[pltpu.VMEM((BUCKET_ROWS, LANES), f32),
                        pltpu.VMEM((TM, TM), bf16)],
        compiler_params=_params("arbitrary"),
        name="outproj_router",
    )(ret, diff, wo, x, g, wr, rb)


def _sc_mesh():
    return plsc.VectorSubcoreMesh(core_axis_name="core", subcore_axis_name="subcore")


def _chunk_rows(pos, chunks, rows_per_chunk):
    return (pos[None, :] + jnp.arange(chunks, dtype=i32)[:, None] * rows_per_chunk).reshape(1, -1)


def _sc_dispatch(pos, xw, rows_out):
    chunks, t, _ = xw.shape
    n = chunks * t

    @functools.partial(pl.kernel, out_type=jax.ShapeDtypeStruct((chunks * rows_out, LANES), xw.dtype),
                       mesh=_sc_mesh(), scratch_types=[])
    def run(x_hbm, i_hbm, o_hbm):
        def body(x_vmem, i_vmem):
            pltpu.sync_copy(x_vmem, o_hbm.at[i_vmem.at[0]])

        pltpu.emit_pipeline(
            body,
            grid=(n // SC_WINDOW,),
            in_specs=[pl.BlockSpec((SC_WINDOW, LANES), lambda i: (i, 0)),
                      pl.BlockSpec((1, SC_WINDOW), lambda i: (0, i))],
            out_specs=[],
            core_axis_name=("core", "subcore"),
            dimension_semantics=(pltpu.PARALLEL,),
        )(x_hbm, i_hbm)

    out = run(xw.reshape(n, LANES), _chunk_rows(pos, chunks, rows_out))
    return out.reshape(chunks, rows_out, LANES)


def _sc_combine(pos, ys, t):
    chunks, rows_in, _ = ys.shape
    n = chunks * t

    @functools.partial(pl.kernel, out_type=jax.ShapeDtypeStruct((n, LANES), ys.dtype),
                       mesh=_sc_mesh(), scratch_types=[])
    def run(y_hbm, i_hbm, o_hbm):
        def body(i_vmem, o_vmem):
            pltpu.sync_copy(y_hbm.at[i_vmem.at[0]], o_vmem)

        pltpu.emit_pipeline(
            body,
            grid=(n // SC_WINDOW,),
            in_specs=[pl.BlockSpec((1, SC_WINDOW), lambda i: (0, i))],
            out_specs=[pl.BlockSpec((SC_WINDOW, LANES), lambda i: (i, 0))],
            core_axis_name=("core", "subcore"),
            dimension_semantics=(pltpu.PARALLEL,),
        )(i_hbm, o_hbm)

    out = run(ys.reshape(chunks * rows_in, LANES), _chunk_rows(pos, chunks, rows_in))
    return out.reshape(chunks, t, LANES)


def _moe_body(e_lo_ref, e_hi_ref, valid_ref, hs_ref, wg1, wu1, wd1, wg2, wu2, wd2, o_ref):
    nv = valid_ref[pl.program_id(0)]

    @pl.when(nv > 0)
    def _():
        ok = lax.broadcasted_iota(i32, (TM_MOE, 1), 0) < nv
        words = jnp.concatenate([hs_ref[c] for c in range(X_CHUNKS)], axis=1)
        h = jnp.where(ok, _unpack_bf16_pairs(words), 0.0).astype(bf16)
        w_lo = jnp.where(ok, hs_ref[X_CHUNKS, :, 0:1], 0.0)
        w_hi = jnp.where(ok, hs_ref[X_CHUNKS, :, 1:2], 0.0)
        cast = lambda w_ref: w_ref[...].astype(bf16)
        hid1 = jax.nn.silu(_dot(h, cast(wg1))) * _dot(h, cast(wu1)) * w_lo
        hid2 = jax.nn.silu(_dot(h, cast(wg2))) * _dot(h, cast(wu2)) * w_hi
        y = _pack_bf16_pairs(_dot(hid1.astype(bf16), cast(wd1)) + _dot(hid2.astype(bf16), cast(wd2)))
        for c in range(X_CHUNKS):
            o_ref[c] = y[:, c * LANES:(c + 1) * LANES]


def _moe(layer, e_lo, e_hi, valid, hs, wg, wu, wd):
    n_tiles = hs.shape[1] // TM_MOE
    up = lambda sel: pl.BlockSpec((None, None, D_MODEL, D_EXPERT),
                                  lambda i, lo, hi, nv: (layer, (lo, hi)[sel][i], 0, 0))
    down = lambda sel: pl.BlockSpec((None, None, D_EXPERT, D_MODEL),
                                    lambda i, lo, hi, nv: (layer, (lo, hi)[sel][i], 0, 0))
    return pl.pallas_call(
        _moe_body,
        out_shape=jax.ShapeDtypeStruct((X_CHUNKS, hs.shape[1], LANES), f32),
        grid_spec=pltpu.PrefetchScalarGridSpec(
            num_scalar_prefetch=3,
            grid=(n_tiles,),
            in_specs=[pl.BlockSpec((ROW_CHUNKS, TM_MOE, LANES), lambda i, lo, hi, nv: (0, jnp.minimum(i, nv[n_tiles]), 0)),
                      up(0), up(0), down(0), up(1), up(1), down(1)],
            out_specs=pl.BlockSpec((X_CHUNKS, TM_MOE, LANES), lambda i, lo, hi, nv: (0, jnp.minimum(i, nv[n_tiles]), 0))),
        compiler_params=_params("arbitrary"),
        name="moe",
    )(e_lo, e_hi, valid, hs, wg, wu, wd, wg, wu, wd)


def _sort_plan(route, counts_f, n_tiles):
    bucket = route[:, 0, :].reshape(-1)
    rank = route[:, 1, :].reshape(-1)
    counts = counts_f[:N_BUCKETS, 0].astype(i32)
    tiles_per = (counts + TM_MOE - 1) // TM_MOE
    tile_end = jnp.cumsum(tiles_per)
    tile_start = tile_end - tiles_per
    pos = (tile_start * TM_MOE)[bucket] + rank
    tile = jnp.arange(n_tiles, dtype=i32)
    used = tile < tile_end[-1]
    owner = lambda tl: jnp.minimum(jnp.sum((tl[:, None] >= tile_end[None, :]).astype(i32), axis=1), N_BUCKETS - 1)
    tb = owner(jnp.where(used, tile, tile_end[-1] - 1))
    valid = jnp.where(used, jnp.clip(counts[tb] - (tile - tile_start[tb]) * TM_MOE, 0, TM_MOE), 0).astype(i32)
    valid = jnp.concatenate([valid, tile_end[-1:] - 1])
    pair_lo = jnp.array([0, 0, 0, 1, 1, 2], i32)
    pair_hi = jnp.array([1, 2, 3, 2, 3, 3], i32)
    grp = tb // N_PAIRS
    e_lo = grp * EXPERTS_PER_GROUP + pair_lo[tb % N_PAIRS]
    e_hi = grp * EXPERTS_PER_GROUP + pair_hi[tb % N_PAIRS]
    return pos.astype(i32), e_lo, e_hi, valid


def _ple_body(x1_ref, y_ref, p_ref, g_ref, wg_ref, wp_ref, gf_ref, o_ref, *, final):
    x = x1_ref[...] + _unpack_bf16_pairs(jnp.concatenate([y_ref[c] for c in range(X_CHUNKS)], axis=1))
    gate = jax.nn.sigmoid(_dot(_rms(x, g_ref[...]).astype(bf16), wg_ref[...]))
    y = x + gate * _dot(p_ref[...].astype(bf16), wp_ref[...])
    o_ref[...] = _rms(y, gf_ref[...]) if final else y


def _ple(layer, x1, y, p, g, wg, wp, gf, final):
    t = x1.shape[0]
    nt = t // TM
    return pl.pallas_call(
        functools.partial(_ple_body, final=final),
        out_shape=jax.ShapeDtypeStruct((t, D_MODEL), f32),
        grid=(nt,),
        in_specs=[pl.BlockSpec((TM, D_MODEL), lambda i: (i, 0)),
                  pl.BlockSpec((X_CHUNKS, TM, LANES), lambda i: (0, i, 0)),
                  pl.BlockSpec((TM, D_PLE), lambda i: (layer * nt + i, 0)),
                  pl.BlockSpec((1, D_MODEL), lambda i: (0, 0)),
                  pl.BlockSpec((D_MODEL, D_MODEL), lambda i: (0, 0)),
                  pl.BlockSpec((D_PLE, D_MODEL), lambda i: (0, 0)),
                  pl.BlockSpec((1, D_MODEL), lambda i: (0, 0))],
        out_specs=pl.BlockSpec((TM, D_MODEL), lambda i: (i, 0)),
        compiler_params=_params("parallel"),
        name="ple_final" if final else "ple",
    )(x1, y, p, g, wg, wp, gf)


def _router_weights(w_rg, b_rg, w_re, b_re):
    w = jnp.zeros((D_MODEL, LANES), f32)
    w = w.at[:, :N_GROUPS].set(w_rg).at[:, 8:8 + N_EXPERTS].set(w_re)
    hi = w.astype(bf16)
    lo = (w - hi.astype(f32)).astype(bf16)
    b = jnp.zeros((1, LANES), f32).at[0, :N_GROUPS].set(b_rg).at[0, 8:8 + N_EXPERTS].set(b_re)
    return jnp.concatenate([hi, lo], axis=1), b


def kernel(x, p, norm_mix, w_in, lam_q1, lam_k1, lam_q2, lam_k2, diff_subln, w_out, norm_ffn,
           w_router_group, b_router_group, w_router_expert, b_router_expert, w_gate, w_up, w_down,
           norm_ple, w_ple_gate, w_ple_proj, norm_final):
    batch, seq, _ = x.shape
    t = batch * seq
    n_tiles = t // TM_MOE + N_BUCKETS
    xf = x.reshape(t, D_MODEL)
    p_rows = p.reshape(DEPTH * t, D_PLE)
    row = lambda v: v.reshape(1, -1).astype(f32)
    for i in range(DEPTH):
        lam_init = 0.8 - 0.6 * math.exp(-0.3 * i)
        proj = _inproj(xf, row(norm_mix[i]), w_in[i].astype(bf16))
        ret = _retention(proj, batch, seq)
        lamv = jnp.zeros((8, LANES), f32)
        for r, v in enumerate((lam_q1[i], lam_k1[i], lam_q2[i], lam_k2[i])):
            lamv = lamv.at[r, :DIFF_DH].set(v.astype(f32))
        consts = jnp.full((1, LANES), lam_init, f32)
        diff = _diffattn(proj, lamv, row(diff_subln[i]), consts, batch, seq)
        wr, rb = _router_weights(w_router_group[i], b_router_group[i], w_router_expert[i], b_router_expert[i])
        x1, hw, route, counts = _outproj(ret, diff, w_out[i].astype(bf16), xf, row(norm_ffn[i]), wr, rb)
        pos, e_lo, e_hi, valid = _sort_plan(route, counts, n_tiles)
        hs = _sc_dispatch(pos, hw, n_tiles * TM_MOE)
        ys = _moe(i, e_lo, e_hi, valid, hs, w_gate, w_up, w_down)
        y = _sc_combine(pos, ys, t)
        xf = _ple(i, x1, y, p_rows, row(norm_ple[i]), w_ple_gate[i].astype(bf16),
                  w_ple_proj[i].astype(bf16), row(norm_final), final=(i == DEPTH - 1))
    return xf.reshape(batch, seq, D_MODEL)
```
